```python
import math
import jax
import jax.numpy as jnp
from jax import lax
import numpy as np

D_MODEL = 2048
BATCH = 1
SEQ = 16384
DEPTH = 4

GRID_W = 64
CTX_LEN = 256
EPS = 1e-6
ROPE_BASE = 10000.0

NA_HEAD_DIM = 128
NA_WIDTH = D_MODEL // 2
NA_HEADS = NA_WIDTH // NA_HEAD_DIM
NA_WIN_H = 8
NA_WIN_W = 16

CONV_WIDTH = D_MODEL - NA_WIDTH
CONV_TAPS = 31

LRU_WIDTH = D_MODEL // 2
LRU_BLOCKS = 8
LRU_BLOCK = LRU_WIDTH // LRU_BLOCKS
LRU_TAPS = 4
LRU_C = 8.0

DIFF_HEAD_DIM = 64
DIFF_WIDTH = D_MODEL - LRU_WIDTH
DIFF_HEADS = DIFF_WIDTH // (2 * DIFF_HEAD_DIM)
Q_BLOCK = 128

MIX_WIDTH = D_MODEL
EVEN_IN = 4 * NA_WIDTH + 3 * CONV_WIDTH
ODD_IN = 2 * LRU_WIDTH + 4 * DIFF_WIDTH

kernel_name = 'hybrid_dit_natten_conformer_rglru_diffattn'


def rms_norm(x, g):
    xf = x.astype(jnp.float32)
    y = xf * lax.rsqrt(jnp.mean(xf * xf, axis=-1, keepdims=True) + EPS)
    return (y * g.astype(jnp.float32)).astype(x.dtype)


def layer_norm(x, g, b):
    xf = x.astype(jnp.float32)
    xc = xf - jnp.mean(xf, axis=-1, keepdims=True)
    var = jnp.mean(xc * xc, axis=-1, keepdims=True)
    return (xc * lax.rsqrt(var + EPS) * g.astype(jnp.float32) + b.astype(jnp.float32)).astype(x.dtype)


def ada_mod(cvec, w, b):
    m = jax.nn.silu(cvec) @ w + b
    return jnp.split(m, 3, axis=-1)


def modulate(x, g, shift, scale):
    return rms_norm(x, g) * (1 + scale) + shift


def split_cols(z, widths):
    offs = np.cumsum(np.array(widths))[:-1].tolist()
    return jnp.split(z, offs, axis=-1)


def depthwise_conv(x, w, b, pad):
    y = lax.conv_general_dilated(x, w[:, None, :].astype(x.dtype), window_strides=(1,),
                                 padding=[pad], dimension_numbers=('NWC', 'WIO', 'NWC'),
                                 feature_group_count=x.shape[-1])
    return y + b


def axial_rope_tables(n_tokens, rot_dim, dtype):
    t = jnp.arange(n_tokens, dtype=jnp.int32)
    row = (t // GRID_W).astype(jnp.float32)
    col = (t % GRID_W).astype(jnp.float32)
    axis_dim = rot_dim // 2
    inv_freq = ROPE_BASE ** (-jnp.arange(0, axis_dim, 2, dtype=jnp.float32) / axis_dim)
    ang_r = row[:, None] * inv_freq
    ang_c = col[:, None] * inv_freq
    ang = jnp.concatenate([ang_r, ang_r, ang_c, ang_c], axis=-1)
    return jnp.cos(ang).astype(dtype), jnp.sin(ang).astype(dtype)


def apply_rope(x, cos, sin):
    x1, x2, x3, x4 = jnp.split(x, 4, axis=-1)
    rotated = jnp.concatenate([-x2, x1, -x4, x3], axis=-1)
    return x * cos + rotated * sin


def softmax_attention(q, k, v):
    s = jnp.einsum('bqhd,bkhd->bhqk', q, k).astype(jnp.float32) * (q.shape[-1] ** -0.5)
    p = jax.nn.softmax(s, axis=-1).astype(v.dtype)
    return jnp.einsum('bhqk,bkhd->bqhd', p, v)


def neighbourhood_attention(q, k, v, kc, vc, rpb):
    B, S, H, Dh = q.shape
    rows = S // GRID_W
    kh, kw = min(NA_WIN_H, rows), NA_WIN_W
    scale = Dh ** -0.5
    qg = q.reshape(B, rows, GRID_W, H, Dh)
    kg = k.reshape(B, rows, GRID_W, H, Dh)
    vg = v.reshape(B, rows, GRID_W, H, Dh)
    cols = np.arange(GRID_W)
    col_idx = np.clip(cols - kw // 2, 0, GRID_W - kw)[:, None] + np.arange(kw)[None, :]
    rpb_c = rpb[:, :, col_idx - cols[:, None] + (NA_WIN_W - 1)]

    def row_block(r):
        rs = jnp.clip(r - kh // 2, 0, rows - kh)
        q_r = lax.dynamic_index_in_dim(qg, r, axis=1, keepdims=False)
        k_win = lax.dynamic_slice_in_dim(kg, rs, kh, axis=1)[:, :, col_idx]
        v_win = lax.dynamic_slice_in_dim(vg, rs, kh, axis=1)[:, :, col_idx]
        bias = jnp.take(rpb_c, rs - r + jnp.arange(kh) + (NA_WIN_H - 1), axis=1)
        s_loc = (jnp.einsum('bqhd,brqwhd->bhqrw', q_r, k_win).astype(jnp.float32) * scale
                 + bias.transpose(0, 2, 1, 3).astype(jnp.float32))
        s_ctx = jnp.einsum('bqhd,bchd->bhqc', q_r, kc).astype(jnp.float32) * scale
        s = jnp.concatenate([s_loc.reshape(B, H, GRID_W, kh * kw), s_ctx], axis=-1)
        p = jax.nn.softmax(s, axis=-1).astype(v.dtype)
        p_loc = p[..., :kh * kw].reshape(B, H, GRID_W, kh, kw)
        return (jnp.einsum('bhqrw,brqwhd->bqhd', p_loc, v_win)
                + jnp.einsum('bhqc,bchd->bqhd', p[..., kh * kw:], vc))

    out = lax.map(row_block, jnp.arange(rows))
    return out.transpose(1, 0, 2, 3, 4).reshape(B, S, H * Dh)


def conformer_conv(u_a, u_b, w, b, ln_g, ln_b):
    u = u_a * jax.nn.sigmoid(u_b)
    u = depthwise_conv(u, w, b, (CONV_TAPS // 2, CONV_TAPS // 2))
    return jax.nn.silu(layer_norm(u, ln_g, ln_b))


def block_diag_linear(x, w, b):
    B, L, C = x.shape
    nb, bw, _ = w.shape
    y = jnp.einsum('blni,nij->blnj', x.reshape(B, L, nb, bw), w)
    return y.reshape(B, L, C) + b


def rglru_coeffs(u, wa, ba, wx, bx, lam):
    r = jax.nn.sigmoid(block_diag_linear(u, wa, ba).astype(jnp.float32))
    i = jax.nn.sigmoid(block_diag_linear(u, wx, bx).astype(jnp.float32))
    log_a = -LRU_C * r * jax.nn.softplus(-lam.astype(jnp.float32))
    a = jnp.exp(log_a)
    b = jnp.sqrt(1.0 - jnp.exp(2.0 * log_a)) * (i * u.astype(jnp.float32))
    return a, b


def _combine(left, right):
    a_l, b_l = left
    a_r, b_r = right
    return a_l * a_r, a_r * b_l + b_r


def linear_scan(a, b, reverse):
    return lax.associative_scan(_combine, (a, b), reverse=reverse, axis=1)


def rglru_direction(u, uc, conv_w, conv_b, wa, ba, wx, bx, lam, reverse):
    pad = (0, LRU_TAPS - 1) if reverse else (LRU_TAPS - 1, 0)
    ul = depthwise_conv(u, conv_w, conv_b, pad)
    ucc = depthwise_conv(uc, conv_w, conv_b, pad)
    ac, bc = rglru_coeffs(ucc, wa, ba, wx, bx, lam)
    _, h_ctx = linear_scan(ac, bc, reverse)
    h_ctx_final = h_ctx[:, 0] if reverse else h_ctx[:, -1]
    al, bl = rglru_coeffs(ul, wa, ba, wx, bx, lam)
    a_cum, h_lat = linear_scan(al, bl, reverse)
    h_lat = h_lat + a_cum * h_ctx_final[:, None, :]
    return h_lat, h_ctx


def diff_block(qb, k, v, lam):
    s = jnp.einsum('bqhmd,bkhmd->bhmqk', qb, k).astype(jnp.float32) * (qb.shape[-1] ** -0.5)
    p = jax.nn.softmax(s, axis=-1)
    a = (p[:, :, 0] - lam * p[:, :, 1]).astype(v.dtype)
    return jnp.einsum('bhqk,bkhe->bqhe', a, v)


def diff_attention_latent(q, k_all, v_all, lam):
    B, S, H, M, Dh = q.shape
    qb = q.reshape(B, S // Q_BLOCK, Q_BLOCK, H, M, Dh).transpose(1, 0, 2, 3, 4, 5)
    o = lax.map(lambda blk: diff_block(blk, k_all, v_all, lam), qb)
    return o.transpose(1, 0, 2, 3, 4).reshape(B, S, H, v_all.shape[-1])


def even_layer(x, ctx, c, c_ctx, ada_w, ada_b, norm_g, w_in, w_out, qn_g, kn_g, rpb,
               cv_w, cv_b, ln_g, ln_b, last):
    B, S, _ = x.shape
    C = ctx.shape[1]
    shift, scale, gate = ada_mod(c, ada_w, ada_b)
    cshift, cscale, cgate = ada_mod(c_ctx, ada_w, ada_b)
    z = modulate(x, norm_g, shift[:, None], scale[:, None]) @ w_in
    zc = modulate(ctx, norm_g, cshift, cscale) @ w_in
    widths = (NA_WIDTH, NA_WIDTH, NA_WIDTH, NA_WIDTH, CONV_WIDTH, CONV_WIDTH, CONV_WIDTH)
    q, k, v, g_a, u_a, u_b, g_b = split_cols(z, widths)
    qc, kc, vc, gc_a, uc_a, uc_b, gc_b = split_cols(zc, widths)

    def heads(t):
        return t.reshape(t.shape[0], t.shape[1], NA_HEADS, NA_HEAD_DIM)

    kc_h = rms_norm(heads(kc), kn_g)
    vc_h = heads(vc)
    y_a = neighbourhood_attention(rms_norm(heads(q), qn_g), rms_norm(heads(k), kn_g), heads(v),
                                  kc_h, vc_h, rpb)
    y_b = conformer_conv(u_a, u_b, cv_w, cv_b, ln_g, ln_b)
    y = jnp.concatenate([y_a * jax.nn.silu(g_a), y_b * jax.nn.silu(g_b)], axis=-1) @ w_out
    x_new = x + gate[:, None] * y
    if last:
        return x_new, ctx
    yc_a = softmax_attention(rms_norm(heads(qc), qn_g), kc_h, vc_h).reshape(B, C, NA_WIDTH)
    yc_b = conformer_conv(uc_a, uc_b, cv_w, cv_b, ln_g, ln_b)
    yc = jnp.concatenate([yc_a * jax.nn.silu(gc_a), yc_b * jax.nn.silu(gc_b)], axis=-1) @ w_out
    return x_new, ctx + cgate * yc


def odd_layer(x, ctx, c, c_ctx, cos, sin, ada_w, ada_b, norm_g, w_in, w_out,
              conv_w, conv_b, wa, ba, wx, bx, lam_a, qn_g, kn_g, lq1, lk1, lq2, lk2, subln_g,
              layer_idx, last):
    B, S, _ = x.shape
    C = ctx.shape[1]
    shift, scale, gate = ada_mod(c, ada_w, ada_b)
    cshift, cscale, cgate = ada_mod(c_ctx, ada_w, ada_b)
    z = modulate(x, norm_g, shift[:, None], scale[:, None]) @ w_in
    zc = modulate(ctx, norm_g, cshift, cscale) @ w_in
    widths = (LRU_WIDTH, LRU_WIDTH, DIFF_WIDTH, DIFF_WIDTH, DIFF_WIDTH, DIFF_WIDTH)
    u, g_c, q, k, v, g_d = split_cols(z, widths)
    uc, gc_c, qc, kc, vc, gc_d = split_cols(zc, widths)

    hf_l, hf_c = rglru_direction(u, uc, conv_w[0], conv_b[0], wa[0], ba[0], wx[0], bx[0], lam_a[0], False)
    hb_l, hb_c = rglru_direction(u, uc, conv_w[1], conv_b[1], wa[1], ba[1], wx[1], bx[1], lam_a[1], True)
    y_c = (hf_l + hb_l).astype(x.dtype)

    def qk_heads(t, g):
        return rms_norm(t.reshape(t.shape[0], t.shape[1], DIFF_HEADS, 2, DIFF_HEAD_DIM), g)

    def v_heads(t):
        return t.reshape(t.shape[0], t.shape[1], DIFF_HEADS, 2 * DIFF_HEAD_DIM)

    cos5, sin5 = cos[None, :, None, None, :], sin[None, :, None, None, :]
    q_h = apply_rope(qk_heads(q, qn_g), cos5, sin5)
    k_h = apply_rope(qk_heads(k, kn_g), cos5, sin5)
    kc_h = qk_heads(kc, kn_g)
    vc_h = v_heads(vc)
    lam_init = 0.8 - 0.6 * math.exp(-0.3 * layer_idx)
    lam = (jnp.exp(jnp.sum(lq1.astype(jnp.float32) * lk1.astype(jnp.float32)))
           - jnp.exp(jnp.sum(lq2.astype(jnp.float32) * lk2.astype(jnp.float32))) + lam_init)
    k_all = jnp.concatenate([k_h, kc_h], axis=1)
    v_all = jnp.concatenate([v_heads(v), vc_h], axis=1)
    o = diff_attention_latent(q_h, k_all, v_all, lam)
    y_d = (rms_norm(o, subln_g) * (1.0 - lam_init)).reshape(B, S, DIFF_WIDTH)

    y = jnp.concatenate([y_c * jax.nn.silu(g_c), y_d * jax.nn.silu(g_d)], axis=-1) @ w_out
    x_new = x + gate[:, None] * y
    if last:
        return x_new, ctx
    yc_c = (hf_c + hb_c).astype(ctx.dtype)
    oc = diff_block(qk_heads(qc, qn_g), kc_h, vc_h, lam)
    yc_d = (rms_norm(oc, subln_g) * (1.0 - lam_init)).reshape(B, C, DIFF_WIDTH)
    yc = jnp.concatenate([yc_c * jax.nn.silu(gc_c), yc_d * jax.nn.silu(gc_d)], axis=-1) @ w_out
    return x_new, ctx + cgate * yc


def setup_inputs(seed: int = 0) -> dict:
    key = jax.random.key(seed)
    ks = jax.random.split(key, 48)
    cnt = [0]

    def nxt():
        k = ks[cnt[0]]
        cnt[0] += 1
        return k

    def nrm(shape, std):
        return jax.random.normal(nxt(), shape, jnp.float32) * std

    ne, no = (DEPTH + 1) // 2, DEPTH // 2
    D = D_MODEL
    ada_std = 0.5 * D ** -0.5
    inp = {
        'x': nrm((BATCH, SEQ, D), 1.0),
        'c': nrm((BATCH, D), 1.0),
        'ctx': nrm((BATCH, CTX_LEN, D), 1.0),
        'c_ctx': nrm((D,), 1.0),
        'ev_ada_w': nrm((ne, D, 3 * D), ada_std),
        'ev_ada_b': nrm((ne, 3 * D), 0.02),
        'ev_norm_g': 1.0 + nrm((ne, D), 0.02),
        'ev_w_in': nrm((ne, D, EVEN_IN), D ** -0.5),
        'ev_w_out': nrm((ne, MIX_WIDTH, D), MIX_WIDTH ** -0.5),
        'ev_na_qn_g': 1.0 + nrm((ne, NA_HEAD_DIM), 0.02),
        'ev_na_kn_g': 1.0 + nrm((ne, NA_HEAD_DIM), 0.02),
        'ev_na_rpb': nrm((ne, NA_HEADS, 2 * NA_WIN_H - 1, 2 * NA_WIN_W - 1), 0.1),
        'ev_cv_w': nrm((ne, CONV_TAPS, CONV_WIDTH), CONV_TAPS ** -0.5),
        'ev_cv_b': nrm((ne, CONV_WIDTH), 0.02),
        'ev_cv_ln_g': 1.0 + nrm((ne, CONV_WIDTH), 0.02),
        'ev_cv_ln_b': nrm((ne, CONV_WIDTH), 0.02),
        'od_ada_w': nrm((no, D, 3 * D), ada_std),
        'od_ada_b': nrm((no, 3 * D), 0.02),
        'od_norm_g': 1.0 + nrm((no, D), 0.02),
        'od_w_in': nrm((no, D, ODD_IN), D ** -0.5),
        'od_w_out': nrm((no, MIX_WIDTH, D), MIX_WIDTH ** -0.5),
        'od_lru_conv_w': nrm((no, 2, LRU_TAPS, LRU_WIDTH), LRU_TAPS ** -0.5),
        'od_lru_conv_b': nrm((no, 2, LRU_WIDTH), 0.02),
        'od_lru_wa': nrm((no, 2, LRU_BLOCKS, LRU_BLOCK, LRU_BLOCK), LRU_BLOCK ** -0.5),
        'od_lru_ba': nrm((no, 2, LRU_WIDTH), 0.02),
        'od_lru_wx': nrm((no, 2, LRU_BLOCKS, LRU_BLOCK, LRU_BLOCK), LRU_BLOCK ** -0.5),
        'od_lru_bx': nrm((no, 2, LRU_WIDTH), 0.02),
    }
    a0 = jnp.sqrt(jax.random.uniform(nxt(), (no, 2, LRU_WIDTH), jnp.float32, 0.81, 0.998))
    inp['od_lru_lam'] = jnp.log(a0) - jnp.log1p(-a0)
    inp['od_df_qn_g'] = 1.0 + nrm((no, DIFF_HEAD_DIM), 0.02)
    inp['od_df_kn_g'] = 1.0 + nrm((no, DIFF_HEAD_DIM), 0.02)
    inp['od_df_lq1'] = nrm((no, DIFF_HEAD_DIM), 0.1)
    inp['od_df_lk1'] = nrm((no, DIFF_HEAD_DIM), 0.1)
    inp['od_df_lq2'] = nrm((no, DIFF_HEAD_DIM), 0.1)
    inp['od_df_lk2'] = nrm((no, DIFF_HEAD_DIM), 0.1)
    inp['od_df_subln_g'] = 1.0 + nrm((no, 2 * DIFF_HEAD_DIM), 0.02)
    return inp


def reference(x, c, ctx, c_ctx,
              ev_ada_w, ev_ada_b, ev_norm_g, ev_w_in, ev_w_out, ev_na_qn_g, ev_na_kn_g, ev_na_rpb,
              ev_cv_w, ev_cv_b, ev_cv_ln_g, ev_cv_ln_b,
              od_ada_w, od_ada_b, od_norm_g, od_w_in, od_w_out, od_lru_conv_w, od_lru_conv_b,
              od_lru_wa, od_lru_ba, od_lru_wx, od_lru_bx, od_lru_lam,
              od_df_qn_g, od_df_kn_g, od_df_lq1, od_df_lk1, od_df_lq2, od_df_lk2, od_df_subln_g):
    cos, sin = axial_rope_tables(x.shape[1], DIFF_HEAD_DIM, x.dtype)
    for l in range(DEPTH):
        last = l == DEPTH - 1
        i = l // 2
        if l % 2 == 0:
            x, ctx = even_layer(x, ctx, c, c_ctx, ev_ada_w[i], ev_ada_b[i], ev_norm_g[i], ev_w_in[i],
                                ev_w_out[i], ev_na_qn_g[i], ev_na_kn_g[i], ev_na_rpb[i],
                                ev_cv_w[i], ev_cv_b[i], ev_cv_ln_g[i], ev_cv_ln_b[i], last)
        else:
            x, ctx = odd_layer(x, ctx, c, c_ctx, cos, sin, od_ada_w[i], od_ada_b[i], od_norm_g[i],
                               od_w_in[i], od_w_out[i], od_lru_conv_w[i], od_lru_conv_b[i],
                               od_lru_wa[i], od_lru_ba[i], od_lru_wx[i], od_lru_bx[i], od_lru_lam[i],
                               od_df_qn_g[i], od_df_kn_g[i], od_df_lq1[i], od_df_lk1[i],
                               od_df_lq2[i], od_df_lk2[i], od_df_subln_g[i], l, last)
    return x
```

```python
import functools
import math

import jax
import jax.numpy as jnp
import numpy as np
from jax import lax
from jax.experimental import pallas as pl
from jax.experimental.pallas import tpu as pltpu

F32 = jnp.float32
BF = jnp.bfloat16

GRID_W = 64
NC = 256
EPS = 1e-6
ROPE_BASE = 10000.0
HEADS = 8
HEAD_W = 128
MIX_HALF = HEADS * HEAD_W
NA_WIN_H, NA_WIN_W = 8, 16
NA_TQ = 256
NA_TK = 768
CONV_TAPS = 31
LRU_TAPS = 4
LRU_C = 8.0
DIFF_DH = 64
HALO = 16
NORM_ROWS = 64
NEG = -1e30
VMEM_LIMIT = 56 * 1024 * 1024


def _pick_tile(n, target, align):
    best = None
    for t in range(align, min(n, target) + 1, align):
        if n % t == 0:
            best = t
    assert best is not None, (n, target, align)
    return best


def _sigmoid(x):
    return 1.0 / (1.0 + jnp.exp(-x))


def _silu(x):
    return x * _sigmoid(x)


def _params(*sem):
    return pltpu.CompilerParams(dimension_semantics=sem, vmem_limit_bytes=VMEM_LIMIT)


def _ada_kernel(cv_ref, w_ref, b_ref, o_ref):
    o_ref[...] = jnp.dot(_silu(cv_ref[...]), w_ref[...], preferred_element_type=F32) + b_ref[...]


def _ada(cv, w, b):
    nl, dm, n3 = w.shape
    tn = n3 // 4
    return pl.pallas_call(
        _ada_kernel,
        grid=(nl, n3 // tn),
        in_specs=[pl.BlockSpec((8, dm), lambda l, j: (0, 0)),
                  pl.BlockSpec((None, dm, tn), lambda l, j: (l, 0, j)),
                  pl.BlockSpec((None, 1, tn), lambda l, j: (l, 0, j))],
        out_specs=pl.BlockSpec((None, 8, tn), lambda l, j: (l, 0, j)),
        out_shape=jax.ShapeDtypeStruct((nl, 8, n3), F32),
        compiler_params=_params("parallel", "parallel"),
        name="ada_mod",
    )(cv, w, b.reshape(nl, 1, n3))


def _row_select(i, tm, n_lat, lat, ctx):
    row = i * tm + lax.broadcasted_iota(jnp.int32, (tm, 1), 0)
    return jnp.where(row < n_lat, lat, ctx)


def _inproj_kernel(x_ref, g_ref, mod_ref, w_ref, o_ref, xm_ref, *, n_lat, tm, dm):
    i = pl.program_id(0)

    @pl.when(pl.program_id(1) == 0)
    def _():
        def chunk(r, carry):
            r0 = pl.multiple_of(r * NORM_ROWS, NORM_ROWS)
            x = x_ref[pl.ds(r0, NORM_ROWS), :]
            ms = jnp.mean(x * x, axis=-1, keepdims=True)
            y = x * lax.rsqrt(ms + EPS) * g_ref[...]
            row = i * tm + r0 + lax.broadcasted_iota(jnp.int32, (NORM_ROWS, 1), 0)
            is_lat = row < n_lat
            shift = jnp.where(is_lat, mod_ref[0:1, 0:dm], mod_ref[1:2, 0:dm])
            scale = jnp.where(is_lat, mod_ref[0:1, dm:2 * dm], mod_ref[1:2, dm:2 * dm])
            xm_ref[pl.ds(r0, NORM_ROWS), :] = (y * (1.0 + scale) + shift).astype(BF)
            return carry
        lax.fori_loop(0, tm // NORM_ROWS, chunk, 0)

    o_ref[...] = jnp.dot(xm_ref[...], w_ref[...], preferred_element_type=F32).astype(BF)


def _inproj(xa, g, mod, w, n_lat):
    s_all, dm = xa.shape
    n = w.shape[1]
    tm = _pick_tile(s_all, 1280, 256)
    tn = 1024
    return pl.pallas_call(
        functools.partial(_inproj_kernel, n_lat=n_lat, tm=tm, dm=dm),
        grid=(s_all // tm, n // tn),
        in_specs=[pl.BlockSpec((tm, dm), lambda i, j: (i, 0)),
                  pl.BlockSpec((1, dm), lambda i, j: (0, 0)),
                  pl.BlockSpec((8, 3 * dm), lambda i, j: (0, 0)),
                  pl.BlockSpec((dm, tn), lambda i, j: (0, j))],
        out_specs=pl.BlockSpec((tm, tn), lambda i, j: (i, j)),
        out_shape=jax.ShapeDtypeStruct((s_all, n), BF),
        scratch_shapes=[pltpu.VMEM((tm, dm), BF)],
        compiler_params=_params("parallel", "arbitrary"),
        name="in_proj",
    )(xa, g.reshape(1, dm), mod, w)


def _outproj_kernel(ya_ref, yb_ref, x_ref, mod_ref, w_ref, o_ref, *, n_lat, tm, dm):
    i = pl.program_id(0)
    y = (jnp.dot(ya_ref[...], w_ref[0:MIX_HALF, :], preferred_element_type=F32)
         + jnp.dot(yb_ref[...], w_ref[MIX_HALF:2 * MIX_HALF, :], preferred_element_type=F32))
    gate = _row_select(i, tm, n_lat, mod_ref[0:1, 2 * dm:3 * dm], mod_ref[1:2, 2 * dm:3 * dm])
    o_ref[...] = x_ref[...] + gate * y


def _outproj(ya, yb, xa, mod, w, n_lat):
    s_all, dm = xa.shape
    tm = _pick_tile(s_all, 640, 16)
    return pl.pallas_call(
        functools.partial(_outproj_kernel, n_lat=n_lat, tm=tm, dm=dm),
        grid=(s_all // tm,),
        in_specs=[pl.BlockSpec((tm, MIX_HALF), lambda i: (i, 0)),
                  pl.BlockSpec((tm, MIX_HALF), lambda i: (i, 0)),
                  pl.BlockSpec((tm, dm), lambda i: (i, 0)),
                  pl.BlockSpec((8, 3 * dm), lambda i: (0, 0)),
                  pl.BlockSpec((2 * MIX_HALF, dm), lambda i: (0, 0), pipeline_mode=pl.Buffered(1))],
        out_specs=pl.BlockSpec((tm, dm), lambda i: (i, 0)),
        out_shape=jax.ShapeDtypeStruct((s_all, dm), F32),
        compiler_params=_params("parallel"),
        name="out_proj",
    )(ya, yb, xa, mod, w)


def _na_bias_table(rpb, rows):
    n_grp = rows // 4
    qc = np.arange(GRID_W)
    cs = np.clip(qc - NA_WIN_W // 2, 0, GRID_W - NA_WIN_W)
    kc = np.arange(GRID_W)
    col_ok = (kc[None, :] >= cs[:, None]) & (kc[None, :] < cs[:, None] + NA_WIN_W)
    dc = np.clip(kc[None, :] - qc[:, None] + (NA_WIN_W - 1), 0, 2 * NA_WIN_W - 2)
    tables = []
    for g0 in (0, 1, n_grp - 1):
        qr = 4 * g0 + np.arange(4)
        ks = int(np.clip(4 * g0 - 4, 0, rows - 12))
        kr = ks + np.arange(12)
        rs = np.clip(qr - NA_WIN_H // 2, 0, rows - NA_WIN_H)
        row_ok = (kr[None, :] >= rs[:, None]) & (kr[None, :] < rs[:, None] + NA_WIN_H)
        dr = np.clip(kr[None, :] - qr[:, None] + (NA_WIN_H - 1), 0, 2 * NA_WIN_H - 2)
        ok = row_ok[:, None, :, None] & col_ok[None, :, None, :]
        vals = rpb[:, dr[:, None, :, None], dc[None, :, None, :]]
        tables.append(jnp.where(ok[None], vals.astype(F32), NEG).reshape(HEADS, NA_TQ, NA_TK))
    tables.append(jnp.full((HEADS, NA_TQ, NA_TK), NEG, F32))
    return jnp.stack(tables, axis=1)


def _na_kernel(q_ref, k_ref, v_ref, g_ref, bias_ref, qg_ref, kg_ref, o_ref, *, n_lat):
    grp = pl.program_id(1)
    ks = pl.multiple_of(jnp.clip((grp - 1) * NA_TQ, 0, n_lat - NA_TK), NA_TQ)

    def norm(t, gain):
        t = t.astype(F32)
        ms = jnp.mean(t * t, axis=-1, keepdims=True)
        return (t * lax.rsqrt(ms + EPS) * gain).astype(BF)

    scale = HEAD_W ** -0.5
    q = norm(q_ref[...], qg_ref[...])
    kl = norm(k_ref[pl.ds(ks, NA_TK), :], kg_ref[...])
    kc = norm(k_ref[n_lat:n_lat + NC, :], kg_ref[...])
    nt = (((1,), (1,)), ((), ()))
    s_loc = lax.dot_general(q, kl, nt, preferred_element_type=F32) * scale + bias_ref[...]
    s_ctx = lax.dot_general(q, kc, nt, preferred_element_type=F32) * scale
    m = jnp.maximum(jnp.max(s_loc, axis=-1, keepdims=True), jnp.max(s_ctx, axis=-1, keepdims=True))
    p_loc = jnp.exp(s_loc - m)
    p_ctx = jnp.exp(s_ctx - m)
    l = jnp.sum(p_loc, axis=-1, keepdims=True) + jnp.sum(p_ctx, axis=-1, keepdims=True)
    o = (jnp.dot(p_loc.astype(BF), v_ref[pl.ds(ks, NA_TK), :], preferred_element_type=F32)
         + jnp.dot(p_ctx.astype(BF), v_ref[n_lat:n_lat + NC, :], preferred_element_type=F32))
    o_ref[...] = ((o / l) * _silu(g_ref[...].astype(F32))).astype(BF)


def _na(z, bias, qn_g, kn_g, n_lat):
    s_all = z.shape[0]
    n_grp = n_lat // NA_TQ

    def variant(h, g):
        return (h, jnp.where(g == 0, 0, jnp.where(g == n_grp - 1, 2, jnp.where(g == n_grp, 3, 1))), 0, 0)

    return pl.pallas_call(
        functools.partial(_na_kernel, n_lat=n_lat),
        grid=(HEADS, s_all // NA_TQ),
        in_specs=[pl.BlockSpec((NA_TQ, HEAD_W), lambda h, g: (g, h)),
                  pl.BlockSpec((s_all, HEAD_W), lambda h, g: (0, HEADS + h)),
                  pl.BlockSpec((s_all, HEAD_W), lambda h, g: (0, 2 * HEADS + h)),
                  pl.BlockSpec((NA_TQ, HEAD_W), lambda h, g: (g, 3 * HEADS + h)),
                  pl.BlockSpec((None, None, NA_TQ, NA_TK), variant),
                  pl.BlockSpec((1, HEAD_W), lambda h, g: (0, 0)),
                  pl.BlockSpec((1, HEAD_W), lambda h, g: (0, 0))],
        out_specs=pl.BlockSpec((NA_TQ, HEAD_W), lambda h, g: (g, h)),
        out_shape=jax.ShapeDtypeStruct((s_all, MIX_HALF), BF),
        compiler_params=_params("parallel", "parallel"),
        name="natten",
    )(z, z, z, z, bias, qn_g.reshape(1, HEAD_W), kn_g.reshape(1, HEAD_W))


def _conv_kernel(ua_ref, ub_ref, uap_ref, ubp_ref, uan_ref, ubn_ref, g_ref, w_ref, b_ref, lg_ref, lb_ref,
                 o_ref, us_ref, *, nt_lat, nt, tt):
    i = pl.program_id(0)
    first = (i == 0) | (i == nt_lat)
    last = (i == nt_lat - 1) | (i == nt - 1)

    def glu(a_ref, b_ref):
        return a_ref[...].astype(F32) * _sigmoid(b_ref[...].astype(F32))

    us_ref[0:HALO, :] = jnp.where(first, 0.0, glu(uap_ref, ubp_ref))
    us_ref[HALO:HALO + tt, :] = glu(ua_ref, ub_ref)
    us_ref[HALO + tt:2 * HALO + tt, :] = jnp.where(last, 0.0, glu(uan_ref, ubn_ref))
    base = HALO - CONV_TAPS // 2
    acc = w_ref[0:1, :] * us_ref[pl.ds(base, tt), :]
    for k in range(1, CONV_TAPS):
        acc = acc + w_ref[k:k + 1, :] * us_ref[pl.ds(base + k, tt), :]
    y = acc + b_ref[...]
    yc = y - jnp.mean(y, axis=-1, keepdims=True)
    var = jnp.mean(yc * yc, axis=-1, keepdims=True)
    yn = yc * lax.rsqrt(var + EPS) * lg_ref[...] + lb_ref[...]
    o_ref[...] = (_silu(yn) * _silu(g_ref[...].astype(F32))).astype(BF)


def _conv(z, w, b, ln_g, ln_b, n_lat):
    s_all = z.shape[0]
    tt = 256
    nt, nt_lat = s_all // tt, n_lat // tt
    hb = tt // HALO
    nh = s_all // HALO
    cw = MIX_HALF
    w_pad = jnp.zeros((32, cw), F32).at[:CONV_TAPS].set(w)
    cur = lambda c: pl.BlockSpec((tt, cw), lambda i: (i, c))
    prev = lambda c: pl.BlockSpec((HALO, cw), lambda i: (jnp.maximum(i * hb - 1, 0), c))
    nxt = lambda c: pl.BlockSpec((HALO, cw), lambda i: (jnp.minimum((i + 1) * hb, nh - 1), c))
    vec = pl.BlockSpec((1, cw), lambda i: (0, 0))
    return pl.pallas_call(
        functools.partial(_conv_kernel, nt_lat=nt_lat, nt=nt, tt=tt),
        grid=(nt,),
        in_specs=[cur(4), cur(5), prev(4), prev(5), nxt(4), nxt(5), cur(6),
                  pl.BlockSpec((32, cw), lambda i: (0, 0)), vec, vec, vec],
        out_specs=pl.BlockSpec((tt, cw), lambda i: (i, 0)),
        out_shape=jax.ShapeDtypeStruct((s_all, cw), BF),
        scratch_shapes=[pltpu.VMEM((tt + 2 * HALO, cw), F32)],
        compiler_params=_params("parallel"),
        name="conformer_conv",
    )(z, z, z, z, z, z, z, w_pad, b.reshape(1, cw), ln_g.reshape(1, cw), ln_b.reshape(1, cw))


def _lru_kernel(*refs, reverse, has_prev, nt_lat, tt):
    if has_prev:
        (u_ref, uh_ref, cw_ref, cb_ref, wa_ref, ba_ref, wx_ref, bx_ref, lam_ref, hp_ref, g_ref,
         o_ref, carry_ref, us_ref, as_ref, bs_ref) = refs
    else:
        (u_ref, uh_ref, cw_ref, cb_ref, wa_ref, ba_ref, wx_ref, bx_ref, lam_ref,
         o_ref, carry_ref, us_ref, as_ref, bs_ref) = refs
    s = pl.program_id(0)
    tile = jnp.where(s == 0, nt_lat, (nt_lat - s) if reverse else (s - 1))

    @pl.when(s == 0)
    def _():
        carry_ref[...] = jnp.zeros_like(carry_ref)

    edge = (tile == nt_lat) | (tile == (nt_lat - 1 if reverse else 0))
    halo = jnp.where(edge, 0.0, uh_ref[...].astype(F32))
    pad = tt // 2
    if reverse:
        us_ref[0:tt, :] = u_ref[...].astype(F32)
        us_ref[tt:tt + HALO, :] = halo
        off = 0
        as_ref[tt:tt + pad, :] = jnp.ones((pad, HEAD_W), F32)
        bs_ref[tt:tt + pad, :] = jnp.zeros((pad, HEAD_W), F32)
        data0 = 0
    else:
        us_ref[0:HALO, :] = halo
        us_ref[HALO:HALO + tt, :] = u_ref[...].astype(F32)
        off = HALO - (LRU_TAPS - 1)
        as_ref[0:pad, :] = jnp.ones((pad, HEAD_W), F32)
        bs_ref[0:pad, :] = jnp.zeros((pad, HEAD_W), F32)
        data0 = pad

    neg_lam = -lam_ref[...]
    softplus = jnp.maximum(neg_lam, 0.0) + jnp.log1p(jnp.exp(-jnp.abs(neg_lam)))
    for blk in range(MIX_HALF // HEAD_W):
        cols = slice(blk * HEAD_W, (blk + 1) * HEAD_W)
        ul = cb_ref[:, cols] + cw_ref[0:1, cols] * us_ref[pl.ds(off, tt), cols]
        for k in range(1, LRU_TAPS):
            ul = ul + cw_ref[k:k + 1, cols] * us_ref[pl.ds(off + k, tt), cols]
        xa = ul.astype(BF)
        r = _sigmoid(jnp.dot(xa, wa_ref[blk], preferred_element_type=F32) + ba_ref[:, cols])
        gi = _sigmoid(jnp.dot(xa, wx_ref[blk], preferred_element_type=F32) + bx_ref[:, cols])
        log_a = -LRU_C * r * softplus[:, cols]
        a = jnp.exp(log_a)
        b = jnp.sqrt(1.0 - jnp.exp(2.0 * log_a)) * (gi * ul)
        d = 1
        while d < tt:
            as_ref[data0:data0 + tt, :] = a
            bs_ref[data0:data0 + tt, :] = b
            sh = data0 + d if reverse else data0 - d
            a_sh = as_ref[pl.ds(sh, tt), :]
            b_sh = bs_ref[pl.ds(sh, tt), :]
            b = a * b_sh + b
            a = a * a_sh
            d *= 2
        h = b + a * carry_ref[0:1, cols]
        h_end = h[0:1, :] if reverse else h[tt - 1:tt, :]
        carry_ref[:, cols] = jnp.broadcast_to(h_end, (8, HEAD_W))
        if has_prev:
            o_ref[:, cols] = ((hp_ref[:, cols] + h) * _silu(g_ref[:, cols].astype(F32))).astype(BF)
        else:
            o_ref[:, cols] = h


def _lru(z, conv_w, conv_b, wa, ba, wx, bx, lam, n_lat, reverse, h_prev=None):
    s_all = z.shape[0]
    tt = 256
    nt, nt_lat = s_all // tt, n_lat // tt
    hb = tt // HALO
    nh = s_all // HALO
    cw = MIX_HALF
    has_prev = h_prev is not None

    def tile(s):
        return jnp.where(s == 0, nt_lat, (nt_lat - s) if reverse else (s - 1))

    if reverse:
        halo_spec = pl.BlockSpec((HALO, cw), lambda s: (jnp.minimum((tile(s) + 1) * hb, nh - 1), 0))
    else:
        halo_spec = pl.BlockSpec((HALO, cw), lambda s: (jnp.maximum(tile(s) * hb - 1, 0), 0))
    vec = pl.BlockSpec((1, cw), lambda s: (0, 0))
    mat = pl.BlockSpec((cw // HEAD_W, HEAD_W, HEAD_W), lambda s: (0, 0, 0))
    in_specs = [pl.BlockSpec((tt, cw), lambda s: (tile(s), 0)), halo_spec,
                pl.BlockSpec((8, cw), lambda s: (0, 0)), vec, mat, vec, mat, vec, vec]
    args = [z, z, jnp.zeros((8, cw), F32).at[:LRU_TAPS].set(conv_w), conv_b.reshape(1, cw),
            wa.astype(BF), ba.reshape(1, cw), wx.astype(BF), bx.reshape(1, cw), lam.reshape(1, cw)]
    if has_prev:
        in_specs += [pl.BlockSpec((tt, cw), lambda s: (tile(s), 0)),
                     pl.BlockSpec((tt, cw), lambda s: (tile(s), 1))]
        args += [h_prev, z]
    return pl.pallas_call(
        functools.partial(_lru_kernel, reverse=reverse, has_prev=has_prev, nt_lat=nt_lat, tt=tt),
        grid=(nt,),
        in_specs=in_specs,
        out_specs=pl.BlockSpec((tt, cw), lambda s: (tile(s), 0)),
        out_shape=jax.ShapeDtypeStruct((s_all, cw), BF if has_prev else F32),
        scratch_shapes=[pltpu.VMEM((8, cw), F32), pltpu.VMEM((tt + HALO, cw), F32),
                        pltpu.VMEM((tt + tt // 2, HEAD_W), F32), pltpu.VMEM((tt + tt // 2, HEAD_W), F32)],
        compiler_params=_params("arbitrary"),
        name="rglru_bwd" if reverse else "rglru_fwd",
    )(*args)


def _rope_tables(n_lat, s_all):
    t = jnp.arange(n_lat, dtype=jnp.int32)
    row = (t // GRID_W).astype(F32)
    col = (t % GRID_W).astype(F32)
    axis_dim = DIFF_DH // 2
    inv_freq = ROPE_BASE ** (-jnp.arange(0, axis_dim, 2, dtype=F32) / axis_dim)
    ang_r = row[:, None] * inv_freq
    ang_c = col[:, None] * inv_freq
    ang = jnp.concatenate([ang_r, ang_r, ang_c, ang_c], axis=-1)
    q4 = DIFF_DH // 4
    sign = np.tile(np.concatenate([-np.ones(q4), np.ones(q4)]), 2).astype(np.float32)
    cos = jnp.tile(jnp.cos(ang), (1, 2))
    sin = jnp.tile(jnp.sin(ang) * sign, (1, 2))
    cos = jnp.concatenate([cos, jnp.ones((s_all - n_lat, HEAD_W), F32)], axis=0)
    sin = jnp.concatenate([sin, jnp.zeros((s_all - n_lat, HEAD_W), F32)], axis=0)
    return cos, sin


def _qkprep_kernel(zq_ref, zk_ref, cos_ref, sin_ref, qg_ref, kg_ref, bd_ref, qp_ref, kp_ref):
    lane = lax.broadcasted_iota(jnp.int32, (1, HEAD_W), 1)
    take_up = (lane % (DIFF_DH // 2)) < (DIFF_DH // 4)
    map0 = lane < DIFF_DH
    cos = cos_ref[...]
    sin = sin_ref[...]
    bd = bd_ref[...]

    def prep(x, gain):
        x = x.astype(F32)
        x2 = x * x
        hi = x2.astype(BF)
        lo = (x2 - hi.astype(F32)).astype(BF)
        ss = jnp.dot(hi, bd, preferred_element_type=F32) + jnp.dot(lo, bd, preferred_element_type=F32)
        xn = x * lax.rsqrt(ss * (1.0 / DIFF_DH) + EPS) * gain
        rot = jnp.where(take_up, pltpu.roll(xn, HEAD_W - DIFF_DH // 4, 1), pltpu.roll(xn, DIFF_DH // 4, 1))
        return xn * cos + rot * sin

    for h in range(HEADS):
        cols = slice(h * HEAD_W, (h + 1) * HEAD_W)
        q = prep(zq_ref[:, cols], qg_ref[...]) * (DIFF_DH ** -0.5)
        qp_ref[0, :, cols] = jnp.where(map0, q, 0.0).astype(BF)
        qp_ref[1, :, cols] = jnp.where(map0, 0.0, q).astype(BF)
        kp_ref[:, cols] = prep(zk_ref[:, cols], kg_ref[...]).astype(BF)


def _qkprep(z, cos, sin, qn_g, kn_g):
    s_all = z.shape[0]
    tt = _pick_tile(s_all, 640, 16)
    cw = MIX_HALF
    bd = jnp.asarray(np.kron(np.eye(2), np.ones((DIFF_DH, DIFF_DH))), BF)
    vec = pl.BlockSpec((1, HEAD_W), lambda i: (0, 0))
    return pl.pallas_call(
        _qkprep_kernel,
        grid=(s_all // tt,),
        in_specs=[pl.BlockSpec((tt, cw), lambda i: (i, 2)),
                  pl.BlockSpec((tt, cw), lambda i: (i, 3)),
                  pl.BlockSpec((tt, HEAD_W), lambda i: (i, 0)),
                  pl.BlockSpec((tt, HEAD_W), lambda i: (i, 0)),
                  vec, vec,
                  pl.BlockSpec((HEAD_W, HEAD_W), lambda i: (0, 0))],
        out_specs=[pl.BlockSpec((2, tt, cw), lambda i: (0, i, 0)),
                   pl.BlockSpec((tt, cw), lambda i: (i, 0))],
        out_shape=[jax.ShapeDtypeStruct((2, s_all, cw), BF), jax.ShapeDtypeStruct((s_all, cw), BF)],
        compiler_params=_params("parallel"),
        name="qk_prep",
    )(z, z, cos, sin, jnp.tile(qn_g, 2).reshape(1, HEAD_W), jnp.tile(kn_g, 2).reshape(1, HEAD_W), bd)


def _dattn_kernel(q_ref, k_ref, v_ref, g_ref, lp_ref, sg_ref, o_ref, *, n_lat, nq, tq, tk, lam_init):
    i = pl.program_id(1)
    q = q_ref[...].reshape(2 * tq, HEAD_W)
    nt = (((1,), (1,)), ((), ()))

    def step(kk, vv, carry):
        m, l, acc = carry
        s = lax.dot_general(q, kk, nt, preferred_element_type=F32)
        m_new = jnp.maximum(m, jnp.max(s, axis=-1, keepdims=True))
        alpha = jnp.exp(m - m_new)
        p = jnp.exp(s - m_new)
        l = alpha * l + jnp.sum(p, axis=-1, keepdims=True)
        acc = alpha * acc + jnp.dot(p.astype(BF), vv, preferred_element_type=F32)
        return m_new, l, acc

    init = (jnp.full((2 * tq, 1), NEG, F32), jnp.zeros((2 * tq, 1), F32), jnp.zeros((2 * tq, HEAD_W), F32))

    def finish(carry):
        _, l, acc = carry
        o = acc / l
        lp = lp_ref[...]
        lam = (jnp.exp(jnp.sum(lp[0:1] * lp[1:2], axis=-1, keepdims=True))
               - jnp.exp(jnp.sum(lp[2:3] * lp[3:4], axis=-1, keepdims=True)) + lam_init)
        o = o[0:tq] - lam * o[tq:2 * tq]
        ms = jnp.mean(o * o, axis=-1, keepdims=True)
        on = o * lax.rsqrt(ms + EPS) * sg_ref[...] * (1.0 - lam_init)
        o_ref[...] = (on * _silu(g_ref[...].astype(F32))).astype(BF)

    @pl.when(i < nq - 1)
    def _():
        def body(c, carry):
            st = pl.multiple_of(c * tk, 256)
            return step(k_ref[pl.ds(st, tk), :], v_ref[pl.ds(st, tk), :], carry)
        finish(lax.fori_loop(0, (n_lat + NC) // tk, body, init))

    @pl.when(i == nq - 1)
    def _():
        finish(step(k_ref[n_lat:n_lat + NC, :], v_ref[n_lat:n_lat + NC, :], init))


def _dattn(qp, kp, z, lq1, lk1, lq2, lk2, subln_g, n_lat, lam_init):
    s_all = z.shape[0]
    tq = NC
    nq = s_all // tq
    tk = _pick_tile(s_all, 1280, 256)
    lp = jnp.zeros((8, HEAD_W), F32).at[0:4, 0:DIFF_DH].set(jnp.stack([lq1, lk1, lq2, lk2]))
    v_col = 4 * HEADS
    g_col = 5 * HEADS
    return pl.pallas_call(
        functools.partial(_dattn_kernel, n_lat=n_lat, nq=nq, tq=tq, tk=tk, lam_init=lam_init),
        grid=(HEADS, nq),
        in_specs=[pl.BlockSpec((2, tq, HEAD_W), lambda h, i: (0, i, h)),
                  pl.BlockSpec((s_all, HEAD_W), lambda h, i: (0, h)),
                  pl.BlockSpec((s_all, HEAD_W), lambda h, i: (0, v_col + h)),
                  pl.BlockSpec((tq, HEAD_W), lambda h, i: (i, g_col + h)),
                  pl.BlockSpec((8, HEAD_W), lambda h, i: (0, 0)),
                  pl.BlockSpec((1, HEAD_W), lambda h, i: (0, 0))],
        out_specs=pl.BlockSpec((tq, HEAD_W), lambda h, i: (i, h)),
        out_shape=jax.ShapeDtypeStruct((s_all, MIX_HALF), BF),
        compiler_params=_params("parallel", "parallel"),
        name="diff_attn",
    )(qp, kp, z, z, lp, subln_g.reshape(1, HEAD_W))


def kernel(x, c, ctx, c_ctx, ev_ada_w, ev_ada_b, ev_norm_g, ev_w_in, ev_w_out, ev_na_qn_g, ev_na_kn_g, ev_na_rpb, ev_cv_w, ev_cv_b, ev_cv_ln_g, ev_cv_ln_b, od_ada_w, od_ada_b, od_norm_g, od_w_in, od_w_out, od_lru_conv_w, od_lru_conv_b, od_lru_wa, od_lru_ba, od_lru_wx, od_lru_bx, od_lru_lam, od_df_qn_g, od_df_kn_g, od_df_lq1, od_df_lk1, od_df_lq2, od_df_lk2, od_df_subln_g):
    batch, n_lat, dm = x.shape
    assert batch == 1 and ctx.shape == (1, NC, dm) and dm == 2 * MIX_HALF
    assert n_lat % (4 * NA_TQ) == 0
    s_all = n_lat + NC
    depth = ev_w_in.shape[0] + od_w_in.shape[0]

    xa = jnp.concatenate([x[0], ctx[0]], axis=0)
    cv = jnp.zeros((8, dm), F32).at[0].set(c[0]).at[1].set(c_ctx)
    mods_ev = _ada(cv, ev_ada_w, ev_ada_b)
    mods_od = _ada(cv, od_ada_w, od_ada_b)
    cos, sin = _rope_tables(n_lat, s_all)

    for l in range(depth):
        i = l // 2
        if l % 2 == 0:
            z = _inproj(xa, ev_norm_g[i], mods_ev[i], ev_w_in[i].astype(BF), n_lat)
            bias = _na_bias_table(ev_na_rpb[i], n_lat // GRID_W)
            ya = _na(z, bias, ev_na_qn_g[i], ev_na_kn_g[i], n_lat)
            yb = _conv(z, ev_cv_w[i], ev_cv_b[i], ev_cv_ln_g[i], ev_cv_ln_b[i], n_lat)
            xa = _outproj(ya, yb, xa, mods_ev[i], ev_w_out[i].astype(BF), n_lat)
        else:
            z = _inproj(xa, od_norm_g[i], mods_od[i], od_w_in[i].astype(BF), n_lat)
            lru = lambda d, rev, hp: _lru(
                z, od_lru_conv_w[i, d], od_lru_conv_b[i, d], od_lru_wa[i, d], od_lru_ba[i, d],
                od_lru_wx[i, d], od_lru_bx[i, d], od_lru_lam[i, d], n_lat, rev, hp)
            ya = lru(1, True, lru(0, False, None))
            qp, kp = _qkprep(z, cos, sin, od_df_qn_g[i], od_df_kn_g[i])
            lam_init = 0.8 - 0.6 * math.exp(-0.3 * l)
            yb = _dattn(qp, kp, z, od_df_lq1[i], od_df_lk1[i], od_df_lq2[i], od_df_lk2[i],
                        od_df_subln_g[i], n_lat, lam_init)
            xa = _outproj(ya, yb, xa, mods_od[i], od_w_out[i].astype(BF), n_lat)
    return xa[:n_lat][None]
```

```python
import functools
import math

import jax
import jax.numpy as jnp
import numpy as np
from jax import lax
from jax.experimental import pallas as pl
from jax.experimental.pallas import tpu as pltpu

F32 = jnp.float32
BF = jnp.bfloat16

GRID_W = 64
NC = 256
EPS = 1e-6
ROPE_BASE = 10000.0
HEADS = 8
HEAD_W = 128
MIX_HALF = HEADS * HEAD_W
NA_WIN_H, NA_WIN_W = 8, 16
NA_TQ = 256
NA_TK = 768
CONV_TAPS = 31
LRU_TAPS = 4
LRU_C = 8.0
DIFF_DH = 64
HALO = 16
NORM_ROWS = 64
NEG = -1e30
Q_SCALE = DIFF_DH ** -0.5 * math.log2(math.e)
VMEM_LIMIT = 56 * 1024 * 1024


def _pick_tile(n, target, align):
    best = None
    for t in range(align, min(n, target) + 1, align):
        if n % t == 0:
            best = t
    assert best is not None, (n, target, align)
    return best


def _sigmoid(x):
    return 1.0 / (1.0 + jnp.exp(-x))


def _silu(x):
    return x * _sigmoid(x)


def _params(*sem):
    return pltpu.CompilerParams(dimension_semantics=sem, vmem_limit_bytes=VMEM_LIMIT)


def _ada_kernel(cv_ref, w_ref, b_ref, o_ref):
    o_ref[...] = jnp.dot(_silu(cv_ref[...]), w_ref[...], preferred_element_type=F32) + b_ref[...]


def _ada(cv, w, b):
    nl, dm, n3 = w.shape
    tn = n3 // 4
    return pl.pallas_call(
        _ada_kernel,
        grid=(nl, n3 // tn),
        in_specs=[pl.BlockSpec((8, dm), lambda l, j: (0, 0)),
                  pl.BlockSpec((None, dm, tn), lambda l, j: (l, 0, j)),
                  pl.BlockSpec((None, 1, tn), lambda l, j: (l, 0, j))],
        out_specs=pl.BlockSpec((None, 8, tn), lambda l, j: (l, 0, j)),
        out_shape=jax.ShapeDtypeStruct((nl, 8, n3), F32),
        compiler_params=_params("parallel", "parallel"),
        name="ada_mod",
    )(cv, w, b.reshape(nl, 1, n3))


def _row_select(i, tm, n_lat, lat, ctx):
    row = i * tm + lax.broadcasted_iota(jnp.int32, (tm, 1), 0)
    return jnp.where(row < n_lat, lat, ctx)


def _inproj_kernel(x_ref, g_ref, mod_ref, w_ref, o_ref, xm_ref, *, n_lat, tm, dm):
    i = pl.program_id(0)

    @pl.when(pl.program_id(1) == 0)
    def _():
        def chunk(r, carry):
            r0 = pl.multiple_of(r * NORM_ROWS, NORM_ROWS)
            x = x_ref[pl.ds(r0, NORM_ROWS), :]
            ms = jnp.mean(x * x, axis=-1, keepdims=True)
            y = x * lax.rsqrt(ms + EPS) * g_ref[...]
            row = i * tm + r0 + lax.broadcasted_iota(jnp.int32, (NORM_ROWS, 1), 0)
            is_lat = row < n_lat
            shift = jnp.where(is_lat, mod_ref[0:1, 0:dm], mod_ref[1:2, 0:dm])
            scale = jnp.where(is_lat, mod_ref[0:1, dm:2 * dm], mod_ref[1:2, dm:2 * dm])
            xm_ref[pl.ds(r0, NORM_ROWS), :] = (y * (1.0 + scale) + shift).astype(BF)
            return carry
        lax.fori_loop(0, tm // NORM_ROWS, chunk, 0)

    o_ref[...] = jnp.dot(xm_ref[...], w_ref[...], preferred_element_type=F32).astype(BF)


def _inproj(xa, g, mod, w, n_lat):
    s_all, dm = xa.shape
    n = w.shape[1]
    tm = _pick_tile(s_all, 1280, 256)
    tn = 1024
    return pl.pallas_call(
        functools.partial(_inproj_kernel, n_lat=n_lat, tm=tm, dm=dm),
        grid=(s_all // tm, n // tn),
        in_specs=[pl.BlockSpec((tm, dm), lambda i, j: (i, 0)),
                  pl.BlockSpec((1, dm), lambda i, j: (0, 0)),
                  pl.BlockSpec((8, 3 * dm), lambda i, j: (0, 0)),
                  pl.BlockSpec((dm, tn), lambda i, j: (0, j))],
        out_specs=pl.BlockSpec((tm, tn), lambda i, j: (i, j)),
        out_shape=jax.ShapeDtypeStruct((s_all, n), BF),
        scratch_shapes=[pltpu.VMEM((tm, dm), BF)],
        compiler_params=_params("parallel", "arbitrary"),
        name="in_proj",
    )(xa, g.reshape(1, dm), mod, w)


def _outproj_kernel(ya_ref, yb_ref, x_ref, mod_ref, w_ref, o_ref, *, n_lat, tm, dm):
    i = pl.program_id(0)
    y = (jnp.dot(ya_ref[...], w_ref[0:MIX_HALF, :], preferred_element_type=F32)
         + jnp.dot(yb_ref[...], w_ref[MIX_HALF:2 * MIX_HALF, :], preferred_element_type=F32))
    gate = _row_select(i, tm, n_lat, mod_ref[0:1, 2 * dm:3 * dm], mod_ref[1:2, 2 * dm:3 * dm])
    o_ref[...] = x_ref[...] + gate * y


def _outproj(ya, yb, xa, mod, w, n_lat):
    s_all, dm = xa.shape
    tm = _pick_tile(s_all, 640, 16)
    return pl.pallas_call(
        functools.partial(_outproj_kernel, n_lat=n_lat, tm=tm, dm=dm),
        grid=(s_all // tm,),
        in_specs=[pl.BlockSpec((tm, MIX_HALF), lambda i: (i, 0)),
                  pl.BlockSpec((tm, MIX_HALF), lambda i: (i, 0)),
                  pl.BlockSpec((tm, dm), lambda i: (i, 0)),
                  pl.BlockSpec((8, 3 * dm), lambda i: (0, 0)),
                  pl.BlockSpec((2 * MIX_HALF, dm), lambda i: (0, 0), pipeline_mode=pl.Buffered(1))],
        out_specs=pl.BlockSpec((tm, dm), lambda i: (i, 0)),
        out_shape=jax.ShapeDtypeStruct((s_all, dm), F32),
        compiler_params=_params("parallel"),
        name="out_proj",
    )(ya, yb, xa, mod, w)


def _na_bias_table(rpb, rows):
    n_grp = rows // 4
    q_rows, k_rows = NA_TQ // GRID_W, NA_TK // GRID_W
    qc = np.arange(GRID_W)
    cs = np.clip(qc - NA_WIN_W // 2, 0, GRID_W - NA_WIN_W)
    kc = np.arange(GRID_W)
    col_ok = (kc[None, :] >= cs[:, None]) & (kc[None, :] < cs[:, None] + NA_WIN_W)
    off = GRID_W - 1 + NA_WIN_W - 1
    rp = jnp.pad(rpb, ((0, 0), (0, 0), (GRID_W - 1, GRID_W - 1)))
    t2 = jnp.stack([rp[:, :, off - c:off - c + GRID_W] for c in range(GRID_W)], axis=2)
    t2 = jnp.where(col_ok, t2.astype(F32), NEG)
    edge = jnp.full((HEADS, k_rows, GRID_W, GRID_W), NEG, F32)
    t2 = jnp.concatenate([edge, t2, edge], axis=1)
    tables = []
    for g0 in (0, 1, n_grp - 1):
        qr = q_rows * g0 + np.arange(q_rows)
        ks = int(np.clip(q_rows * g0 - NA_WIN_H // 2, 0, rows - k_rows))
        kr = ks + np.arange(k_rows)
        rs = np.clip(qr - NA_WIN_H // 2, 0, rows - NA_WIN_H)
        row_ok = (kr[None, :] >= rs[:, None]) & (kr[None, :] < rs[:, None] + NA_WIN_H)
        per_row = []
        for a in range(q_rows):
            lo = ks - int(qr[a]) + (NA_WIN_H - 1) + k_rows
            blk = jnp.where(row_ok[a][None, :, None, None], t2[:, lo:lo + k_rows], NEG)
            per_row.append(blk.transpose(0, 2, 1, 3))
        tables.append(jnp.stack(per_row, axis=1).reshape(HEADS, NA_TQ, NA_TK))
    tables.append(jnp.full((HEADS, NA_TQ, NA_TK), NEG, F32))
    return jnp.stack(tables, axis=1)


def _na_kernel(q_ref, k_ref, v_ref, g_ref, bias_ref, qg_ref, kg_ref, o_ref, *, n_lat):
    grp = pl.program_id(1)
    ks = pl.multiple_of(jnp.clip((grp - 1) * NA_TQ, 0, n_lat - NA_TK), NA_TQ)

    def norm(t, gain):
        t = t.astype(F32)
        ms = jnp.mean(t * t, axis=-1, keepdims=True)
        return (t * lax.rsqrt(ms + EPS) * gain).astype(BF)

    scale = HEAD_W ** -0.5
    q = norm(q_ref[...], qg_ref[...])
    kl = norm(k_ref[pl.ds(ks, NA_TK), :], kg_ref[...])
    kc = norm(k_ref[n_lat:n_lat + NC, :], kg_ref[...])
    nt = (((1,), (1,)), ((), ()))
    s_loc = lax.dot_general(q, kl, nt, preferred_element_type=F32) * scale + bias_ref[...]
    s_ctx = lax.dot_general(q, kc, nt, preferred_element_type=F32) * scale
    m = jnp.maximum(jnp.max(s_loc, axis=-1, keepdims=True), jnp.max(s_ctx, axis=-1, keepdims=True))
    p_loc = jnp.exp(s_loc - m)
    p_ctx = jnp.exp(s_ctx - m)
    l = jnp.sum(p_loc, axis=-1, keepdims=True) + jnp.sum(p_ctx, axis=-1, keepdims=True)
    o = (jnp.dot(p_loc.astype(BF), v_ref[pl.ds(ks, NA_TK), :], preferred_element_type=F32)
         + jnp.dot(p_ctx.astype(BF), v_ref[n_lat:n_lat + NC, :], preferred_element_type=F32))
    o_ref[...] = ((o / l) * _silu(g_ref[...].astype(F32))).astype(BF)


def _na(z, bias, qn_g, kn_g, n_lat):
    s_all = z.shape[0]
    n_grp = n_lat // NA_TQ

    def variant(h, g):
        return (h, jnp.where(g == 0, 0, jnp.where(g == n_grp - 1, 2, jnp.where(g == n_grp, 3, 1))), 0, 0)

    return pl.pallas_call(
        functools.partial(_na_kernel, n_lat=n_lat),
        grid=(HEADS, s_all // NA_TQ),
        in_specs=[pl.BlockSpec((NA_TQ, HEAD_W), lambda h, g: (g, h)),
                  pl.BlockSpec((s_all, HEAD_W), lambda h, g: (0, HEADS + h)),
                  pl.BlockSpec((s_all, HEAD_W), lambda h, g: (0, 2 * HEADS + h)),
                  pl.BlockSpec((NA_TQ, HEAD_W), lambda h, g: (g, 3 * HEADS + h)),
                  pl.BlockSpec((None, None, NA_TQ, NA_TK), variant),
                  pl.BlockSpec((1, HEAD_W), lambda h, g: (0, 0)),
                  pl.BlockSpec((1, HEAD_W), lambda h, g: (0, 0))],
        out_specs=pl.BlockSpec((NA_TQ, HEAD_W), lambda h, g: (g, h)),
        out_shape=jax.ShapeDtypeStruct((s_all, MIX_HALF), BF),
        compiler_params=_params("parallel", "parallel"),
        name="natten",
    )(z, z, z, z, bias, qn_g.reshape(1, HEAD_W), kn_g.reshape(1, HEAD_W))


def _conv_kernel(ua_ref, ub_ref, uap_ref, ubp_ref, uan_ref, ubn_ref, g_ref, w_ref, b_ref, lg_ref, lb_ref,
                 o_ref, us_ref, *, nt_lat, nt, tt):
    i = pl.program_id(0)
    first = (i == 0) | (i == nt_lat)
    last = (i == nt_lat - 1) | (i == nt - 1)

    def glu(a_ref, b_ref):
        return a_ref[...].astype(F32) * _sigmoid(b_ref[...].astype(F32))

    us_ref[0:HALO, :] = jnp.where(first, 0.0, glu(uap_ref, ubp_ref))
    us_ref[HALO:HALO + tt, :] = glu(ua_ref, ub_ref)
    us_ref[HALO + tt:2 * HALO + tt, :] = jnp.where(last, 0.0, glu(uan_ref, ubn_ref))
    base = HALO - CONV_TAPS // 2
    acc = w_ref[0:1, :] * us_ref[pl.ds(base, tt), :]
    for k in range(1, CONV_TAPS):
        acc = acc + w_ref[k:k + 1, :] * us_ref[pl.ds(base + k, tt), :]
    y = acc + b_ref[...]
    yc = y - jnp.mean(y, axis=-1, keepdims=True)
    var = jnp.mean(yc * yc, axis=-1, keepdims=True)
    yn = yc * lax.rsqrt(var + EPS) * lg_ref[...] + lb_ref[...]
    o_ref[...] = (_silu(yn) * _silu(g_ref[...].astype(F32))).astype(BF)


def _conv(z, w, b, ln_g, ln_b, n_lat):
    s_all = z.shape[0]
    tt = 256
    nt, nt_lat = s_all // tt, n_lat // tt
    hb = tt // HALO
    nh = s_all // HALO
    cw = MIX_HALF
    w_pad = jnp.zeros((32, cw), F32).at[:CONV_TAPS].set(w)
    cur = lambda c: pl.BlockSpec((tt, cw), lambda i: (i, c))
    prev = lambda c: pl.BlockSpec((HALO, cw), lambda i: (jnp.maximum(i * hb - 1, 0), c))
    nxt = lambda c: pl.BlockSpec((HALO, cw), lambda i: (jnp.minimum((i + 1) * hb, nh - 1), c))
    vec = pl.BlockSpec((1, cw), lambda i: (0, 0))
    return pl.pallas_call(
        functools.partial(_conv_kernel, nt_lat=nt_lat, nt=nt, tt=tt),
        grid=(nt,),
        in_specs=[cur(4), cur(5), prev(4), prev(5), nxt(4), nxt(5), cur(6),
                  pl.BlockSpec((32, cw), lambda i: (0, 0)), vec, vec, vec],
        out_specs=pl.BlockSpec((tt, cw), lambda i: (i, 0)),
        out_shape=jax.ShapeDtypeStruct((s_all, cw), BF),
        scratch_shapes=[pltpu.VMEM((tt + 2 * HALO, cw), F32)],
        compiler_params=_params("parallel"),
        name="conformer_conv",
    )(z, z, z, z, z, z, z, w_pad, b.reshape(1, cw), ln_g.reshape(1, cw), ln_b.reshape(1, cw))


def _lru_kernel(*refs, reverse, has_prev, nt_lat, tt):
    if has_prev:
        (u_ref, uh_ref, cw_ref, cb_ref, wa_ref, ba_ref, wx_ref, bx_ref, lam_ref, hp_ref, g_ref,
         o_ref, carry_ref, us_ref, as_ref, bs_ref) = refs
    else:
        (u_ref, uh_ref, cw_ref, cb_ref, wa_ref, ba_ref, wx_ref, bx_ref, lam_ref,
         o_ref, carry_ref, us_ref, as_ref, bs_ref) = refs
    s = pl.program_id(0)
    tile = jnp.where(s == 0, nt_lat, (nt_lat - s) if reverse else (s - 1))

    @pl.when(s == 0)
    def _():
        carry_ref[...] = jnp.zeros_like(carry_ref)

    edge = (tile == nt_lat) | (tile == (nt_lat - 1 if reverse else 0))
    halo = jnp.where(edge, 0.0, uh_ref[...].astype(F32))
    pad = tt // 2
    if reverse:
        us_ref[0:tt, :] = u_ref[...].astype(F32)
        us_ref[tt:tt + HALO, :] = halo
        off = 0
        as_ref[tt:tt + pad, :] = jnp.ones((pad, HEAD_W), F32)
        bs_ref[tt:tt + pad, :] = jnp.zeros((pad, HEAD_W), F32)
        data0 = 0
    else:
        us_ref[0:HALO, :] = halo
        us_ref[HALO:HALO + tt, :] = u_ref[...].astype(F32)
        off = HALO - (LRU_TAPS - 1)
        as_ref[0:pad, :] = jnp.ones((pad, HEAD_W), F32)
        bs_ref[0:pad, :] = jnp.zeros((pad, HEAD_W), F32)
        data0 = pad

    neg_lam = -lam_ref[...]
    softplus = jnp.maximum(neg_lam, 0.0) + jnp.log1p(jnp.exp(-jnp.abs(neg_lam)))
    for blk in range(MIX_HALF // HEAD_W):
        cols = slice(blk * HEAD_W, (blk + 1) * HEAD_W)
        ul = cb_ref[:, cols] + cw_ref[0:1, cols] * us_ref[pl.ds(off, tt), cols]
        for k in range(1, LRU_TAPS):
            ul = ul + cw_ref[k:k + 1, cols] * us_ref[pl.ds(off + k, tt), cols]
        xa = ul.astype(BF)
        r = _sigmoid(jnp.dot(xa, wa_ref[blk], preferred_element_type=F32) + ba_ref[:, cols])
        gi = _sigmoid(jnp.dot(xa, wx_ref[blk], preferred_element_type=F32) + bx_ref[:, cols])
        log_a = -LRU_C * r * softplus[:, cols]
        a = jnp.exp(log_a)
        b = jnp.sqrt(1.0 - jnp.exp(2.0 * log_a)) * (gi * ul)
        d = 1
        while d < tt:
            as_ref[data0:data0 + tt, :] = a
            bs_ref[data0:data0 + tt, :] = b
            sh = data0 + d if reverse else data0 - d
            a_sh = as_ref[pl.ds(sh, tt), :]
            b_sh = bs_ref[pl.ds(sh, tt), :]
            b = a * b_sh + b
            a = a * a_sh
            d *= 2
        h = b + a * carry_ref[0:1, cols]
        h_end = h[0:1, :] if reverse else h[tt - 1:tt, :]
        carry_ref[:, cols] = jnp.broadcast_to(h_end, (8, HEAD_W))
        if has_prev:
            o_ref[:, cols] = ((hp_ref[:, cols] + h) * _silu(g_ref[:, cols].astype(F32))).astype(BF)
        else:
            o_ref[:, cols] = h


def _lru(z, conv_w, conv_b, wa, ba, wx, bx, lam, n_lat, reverse, h_prev=None):
    s_all = z.shape[0]
    tt = 256
    nt, nt_lat = s_all // tt, n_lat // tt
    hb = tt // HALO
    nh = s_all // HALO
    cw = MIX_HALF
    has_prev = h_prev is not None

    def tile(s):
        return jnp.where(s == 0, nt_lat, (nt_lat - s) if reverse else (s - 1))

    if reverse:
        halo_spec = pl.BlockSpec((HALO, cw), lambda s: (jnp.minimum((tile(s) + 1) * hb, nh - 1), 0))
    else:
        halo_spec = pl.BlockSpec((HALO, cw), lambda s: (jnp.maximum(tile(s) * hb - 1, 0), 0))
    vec = pl.BlockSpec((1, cw), lambda s: (0, 0))
    mat = pl.BlockSpec((cw // HEAD_W, HEAD_W, HEAD_W), lambda s: (0, 0, 0))
    in_specs = [pl.BlockSpec((tt, cw), lambda s: (tile(s), 0)), halo_spec,
                pl.BlockSpec((8, cw), lambda s: (0, 0)), vec, mat, vec, mat, vec, vec]
    args = [z, z, jnp.zeros((8, cw), F32).at[:LRU_TAPS].set(conv_w), conv_b.reshape(1, cw),
            wa.astype(BF), ba.reshape(1, cw), wx.astype(BF), bx.reshape(1, cw), lam.reshape(1, cw)]
    if has_prev:
        in_specs += [pl.BlockSpec((tt, cw), lambda s: (tile(s), 0)),
                     pl.BlockSpec((tt, cw), lambda s: (tile(s), 1))]
        args += [h_prev, z]
    return pl.pallas_call(
        functools.partial(_lru_kernel, reverse=reverse, has_prev=has_prev, nt_lat=nt_lat, tt=tt),
        grid=(nt,),
        in_specs=in_specs,
        out_specs=pl.BlockSpec((tt, cw), lambda s: (tile(s), 0)),
        out_shape=jax.ShapeDtypeStruct((s_all, cw), BF if has_prev else F32),
        scratch_shapes=[pltpu.VMEM((8, cw), F32), pltpu.VMEM((tt + HALO, cw), F32),
                        pltpu.VMEM((tt + tt // 2, HEAD_W), F32), pltpu.VMEM((tt + tt // 2, HEAD_W), F32)],
        compiler_params=_params("arbitrary"),
        name="rglru_bwd" if reverse else "rglru_fwd",
    )(*args)


def _rope_tables(n_lat, s_all):
    t = jnp.arange(n_lat, dtype=jnp.int32)
    row = (t // GRID_W).astype(F32)
    col = (t % GRID_W).astype(F32)
    axis_dim = DIFF_DH // 2
    inv_freq = ROPE_BASE ** (-jnp.arange(0, axis_dim, 2, dtype=F32) / axis_dim)
    ang_r = row[:, None] * inv_freq
    ang_c = col[:, None] * inv_freq
    ang = jnp.concatenate([ang_r, ang_r, ang_c, ang_c], axis=-1)
    q4 = DIFF_DH // 4
    sign = np.tile(np.concatenate([-np.ones(q4), np.ones(q4)]), 2).astype(np.float32)
    cos = jnp.tile(jnp.cos(ang), (1, 2))
    sin = jnp.tile(jnp.sin(ang) * sign, (1, 2))
    cos = jnp.concatenate([cos, jnp.ones((s_all - n_lat, HEAD_W), F32)], axis=0)
    sin = jnp.concatenate([sin, jnp.zeros((s_all - n_lat, HEAD_W), F32)], axis=0)
    return cos, sin


def _qkprep_kernel(zq_ref, zk_ref, cos_ref, sin_ref, qg_ref, kg_ref, bd_ref, qp_ref, kp_ref):
    lane = lax.broadcasted_iota(jnp.int32, (1, HEAD_W), 1)
    take_up = (lane % (DIFF_DH // 2)) < (DIFF_DH // 4)
    map0 = lane < DIFF_DH
    cos = cos_ref[...]
    sin = sin_ref[...]
    bd = bd_ref[...]

    def prep(x, gain):
        x = x.astype(F32)
        x2 = x * x
        hi = x2.astype(BF)
        lo = (x2 - hi.astype(F32)).astype(BF)
        ss = jnp.dot(hi, bd, preferred_element_type=F32) + jnp.dot(lo, bd, preferred_element_type=F32)
        xn = x * lax.rsqrt(ss * (1.0 / DIFF_DH) + EPS) * gain
        rot = jnp.where(take_up, pltpu.roll(xn, HEAD_W - DIFF_DH // 4, 1), pltpu.roll(xn, DIFF_DH // 4, 1))
        return xn * cos + rot * sin

    for h in range(HEADS):
        cols = slice(h * HEAD_W, (h + 1) * HEAD_W)
        q = prep(zq_ref[:, cols], qg_ref[...]) * Q_SCALE
        qp_ref[0, :, cols] = jnp.where(map0, q, 0.0).astype(BF)
        qp_ref[1, :, cols] = jnp.where(map0, 0.0, q).astype(BF)
        kp_ref[:, cols] = prep(zk_ref[:, cols], kg_ref[...]).astype(BF)


def _qkprep(z, cos, sin, qn_g, kn_g):
    s_all = z.shape[0]
    tt = _pick_tile(s_all, 640, 16)
    cw = MIX_HALF
    bd = jnp.asarray(np.kron(np.eye(2), np.ones((DIFF_DH, DIFF_DH))), BF)
    vec = pl.BlockSpec((1, HEAD_W), lambda i: (0, 0))
    return pl.pallas_call(
        _qkprep_kernel,
        grid=(s_all // tt,),
        in_specs=[pl.BlockSpec((tt, cw), lambda i: (i, 2)),
                  pl.BlockSpec((tt, cw), lambda i: (i, 3)),
                  pl.BlockSpec((tt, HEAD_W), lambda i: (i, 0)),
                  pl.BlockSpec((tt, HEAD_W), lambda i: (i, 0)),
                  vec, vec,
                  pl.BlockSpec((HEAD_W, HEAD_W), lambda i: (0, 0))],
        out_specs=[pl.BlockSpec((2, tt, cw), lambda i: (0, i, 0)),
                   pl.BlockSpec((tt, cw), lambda i: (i, 0))],
        out_shape=[jax.ShapeDtypeStruct((2, s_all, cw), BF), jax.ShapeDtypeStruct((s_all, cw), BF)],
        compiler_params=_params("parallel"),
        name="qk_prep",
    )(z, z, cos, sin, jnp.tile(qn_g, 2).reshape(1, HEAD_W), jnp.tile(kn_g, 2).reshape(1, HEAD_W), bd)


def _dattn_kernel(q_ref, k_ref, kc_ref, vt_ref, vtc_ref, g_ref, lp_ref, sg_ref, o_ref, s_ref, mx_ref, *,
                  nq, tq, lam_init):
    i = pl.program_id(1)
    q = q_ref[...].reshape(2 * tq, HEAD_W)
    nt = (((1,), (1,)), ((), ()))

    def scores(kk, slot):
        s = lax.dot_general(kk, q, nt, preferred_element_type=F32)
        s_ref[slot, 0:s.shape[0], :] = s
        mx_ref[slot] = jnp.max(s, axis=0, keepdims=True)

    def accumulate(vt, slot, carry):
        m, l, acc = carry
        m_new = jnp.maximum(m, mx_ref[slot])
        alpha = jnp.exp2(m - m_new)
        p = jnp.exp2(s_ref[slot, 0:vt.shape[1], :] - m_new)
        l = alpha * l + jnp.sum(p, axis=0, keepdims=True)
        acc = alpha * acc + jnp.dot(vt, p.astype(BF), preferred_element_type=F32)
        return m_new, l, acc

    init = (jnp.full((1, 2 * tq), NEG, F32), jnp.zeros((1, 2 * tq), F32), jnp.zeros((HEAD_W, 2 * tq), F32))

    def finish(carry):
        _, l, acc = carry
        o = (acc / l).T
        lp = lp_ref[...]
        lam = (jnp.exp(jnp.sum(lp[0:1] * lp[1:2], axis=-1, keepdims=True))
               - jnp.exp(jnp.sum(lp[2:3] * lp[3:4], axis=-1, keepdims=True)) + lam_init)
        o = o[0:tq] - lam * o[tq:2 * tq]
        ms = jnp.mean(o * o, axis=-1, keepdims=True)
        on = o * lax.rsqrt(ms + EPS) * sg_ref[...] * (1.0 - lam_init)
        o_ref[...] = (on * _silu(g_ref[...].astype(F32))).astype(BF)

    @pl.when(i < nq - 1)
    def _():
        n_chunk = k_ref.shape[0]
        scores(k_ref[0], 0)

        def pair(j, carry):
            c = 2 * j
            scores(k_ref[c + 1], 1)
            carry = accumulate(vt_ref[c], 0, carry)
            scores(k_ref[c + 2], 0)
            return accumulate(vt_ref[c + 1], 1, carry)

        carry = lax.fori_loop(0, (n_chunk - 1) // 2, pair, init)
        if n_chunk % 2 == 0:
            scores(k_ref[n_chunk - 1], 1)
            carry = accumulate(vt_ref[n_chunk - 2], 0, carry)
            carry = accumulate(vt_ref[n_chunk - 1], 1, carry)
        else:
            carry = accumulate(vt_ref[n_chunk - 1], 0, carry)
        finish(carry)

    @pl.when(i == nq - 1)
    def _():
        scores(kc_ref[...], 0)
        finish(accumulate(vtc_ref[...], 0, init))


def _dattn(qp, kp, z, lq1, lk1, lq2, lk2, subln_g, n_lat, lam_init):
    s_all = z.shape[0]
    tq = NC
    nq = s_all // tq
    tk = _pick_tile(s_all, 1280, 256)
    n_chunk = s_all // tk
    lp = jnp.zeros((8, HEAD_W), F32).at[0:4, 0:DIFF_DH].set(jnp.stack([lq1, lk1, lq2, lk2]))
    k4 = kp.reshape(n_chunk, tk, HEADS, HEAD_W).transpose(2, 0, 1, 3)
    v = z[:, 4 * MIX_HALF:5 * MIX_HALF]
    vt4 = v.reshape(n_chunk, tk, HEADS, HEAD_W).transpose(2, 0, 3, 1)
    kc = kp[n_lat:].reshape(NC, HEADS, HEAD_W).transpose(1, 0, 2)
    vtc = v[n_lat:].reshape(NC, HEADS, HEAD_W).transpose(1, 2, 0)
    g_col = 5 * HEADS
    return pl.pallas_call(
        functools.partial(_dattn_kernel, nq=nq, tq=tq, lam_init=lam_init),
        grid=(HEADS, nq),
        in_specs=[pl.BlockSpec((2, tq, HEAD_W), lambda h, i: (0, i, h)),
                  pl.BlockSpec((None, n_chunk, tk, HEAD_W), lambda h, i: (h, 0, 0, 0)),
                  pl.BlockSpec((None, NC, HEAD_W), lambda h, i: (h, 0, 0)),
                  pl.BlockSpec((None, n_chunk, HEAD_W, tk), lambda h, i: (h, 0, 0, 0)),
                  pl.BlockSpec((None, HEAD_W, NC), lambda h, i: (h, 0, 0)),
                  pl.BlockSpec((tq, HEAD_W), lambda h, i: (i, g_col + h)),
                  pl.BlockSpec((8, HEAD_W), lambda h, i: (0, 0)),
                  pl.BlockSpec((1, HEAD_W), lambda h, i: (0, 0))],
        out_specs=pl.BlockSpec((tq, HEAD_W), lambda h, i: (i, h)),
        out_shape=jax.ShapeDtypeStruct((s_all, MIX_HALF), BF),
        scratch_shapes=[pltpu.VMEM((2, tk, 2 * tq), F32), pltpu.VMEM((2, 1, 2 * tq), F32)],
        compiler_params=_params("parallel", "parallel"),
        name="diff_attn",
    )(qp, k4, kc, vt4, vtc, z, lp, subln_g.reshape(1, HEAD_W))


def kernel(x, c, ctx, c_ctx, ev_ada_w, ev_ada_b, ev_norm_g, ev_w_in, ev_w_out, ev_na_qn_g, ev_na_kn_g, ev_na_rpb, ev_cv_w, ev_cv_b, ev_cv_ln_g, ev_cv_ln_b, od_ada_w, od_ada_b, od_norm_g, od_w_in, od_w_out, od_lru_conv_w, od_lru_conv_b, od_lru_wa, od_lru_ba, od_lru_wx, od_lru_bx, od_lru_lam, od_df_qn_g, od_df_kn_g, od_df_lq1, od_df_lk1, od_df_lq2, od_df_lk2, od_df_subln_g):
    batch, n_lat, dm = x.shape
    assert batch == 1 and ctx.shape == (1, NC, dm) and dm == 2 * MIX_HALF
    assert n_lat % (4 * NA_TQ) == 0
    s_all = n_lat + NC
    depth = ev_w_in.shape[0] + od_w_in.shape[0]

    xa = jnp.concatenate([x[0], ctx[0]], axis=0)
    cv = jnp.zeros((8, dm), F32).at[0].set(c[0]).at[1].set(c_ctx)
    mods_ev = _ada(cv, ev_ada_w, ev_ada_b)
    mods_od = _ada(cv, od_ada_w, od_ada_b)
    cos, sin = _rope_tables(n_lat, s_all)

    for l in range(depth):
        i = l // 2
        if l % 2 == 0:
            z = _inproj(xa, ev_norm_g[i], mods_ev[i], ev_w_in[i].astype(BF), n_lat)
            bias = _na_bias_table(ev_na_rpb[i], n_lat // GRID_W)
            ya = _na(z, bias, ev_na_qn_g[i], ev_na_kn_g[i], n_lat)
            yb = _conv(z, ev_cv_w[i], ev_cv_b[i], ev_cv_ln_g[i], ev_cv_ln_b[i], n_lat)
            xa = _outproj(ya, yb, xa, mods_ev[i], ev_w_out[i].astype(BF), n_lat)
        else:
            z = _inproj(xa, od_norm_g[i], mods_od[i], od_w_in[i].astype(BF), n_lat)
            lru = lambda d, rev, hp: _lru(
                z, od_lru_conv_w[i, d], od_lru_conv_b[i, d], od_lru_wa[i, d], od_lru_ba[i, d],
                od_lru_wx[i, d], od_lru_bx[i, d], od_lru_lam[i, d], n_lat, rev, hp)
            ya = lru(1, True, lru(0, False, None))
            qp, kp = _qkprep(z, cos, sin, od_df_qn_g[i], od_df_kn_g[i])
            lam_init = 0.8 - 0.6 * math.exp(-0.3 * l)
            yb = _dattn(qp, kp, z, od_df_lq1[i], od_df_lk1[i], od_df_lq2[i], od_df_lk2[i],
                        od_df_subln_g[i], n_lat, lam_init)
            xa = _outproj(ya, yb, xa, mods_od[i], od_w_out[i].astype(BF), n_lat)
    return xa[:n_lat][None]
```

```python
import functools
import math

import jax
import jax.numpy as jnp
import numpy as np
from jax import lax
from jax.experimental import pallas as pl
from jax.experimental.pallas import tpu as pltpu

F32 = jnp.float32
BF = jnp.bfloat16

GRID_W = 64
NC = 256
EPS = 1e-6
ROPE_BASE = 10000.0
HEADS = 8
HEAD_W = 128
MIX_HALF = HEADS * HEAD_W
NA_WIN_H, NA_WIN_W = 8, 16
NA_TQ = 256
NA_TK = 768
CONV_TAPS = 31
LRU_TAPS = 4
LRU_C = 8.0
DIFF_DH = 64
HALO = 16
NORM_ROWS = 64
NEG = -1e30
Q_SCALE = DIFF_DH ** -0.5 * math.log2(math.e)
VMEM_LIMIT = 56 * 1024 * 1024


def _pick_tile(n, target, align):
    best = None
    for t in range(align, min(n, target) + 1, align):
        if n % t == 0:
            best = t
    assert best is not None, (n, target, align)
    return best


def _sigmoid(x):
    return 1.0 / (1.0 + jnp.exp(-x))


def _silu(x):
    return x * _sigmoid(x)


def _params(*sem):
    return pltpu.CompilerParams(dimension_semantics=sem, vmem_limit_bytes=VMEM_LIMIT)


def _ada_kernel(cv_ref, w_ref, b_ref, o_ref):
    o_ref[...] = jnp.dot(_silu(cv_ref[...]), w_ref[...], preferred_element_type=F32) + b_ref[...]


def _ada(cv, w, b):
    nl, dm, n3 = w.shape
    tn = n3 // 4
    return pl.pallas_call(
        _ada_kernel,
        grid=(nl, n3 // tn),
        in_specs=[pl.BlockSpec((8, dm), lambda l, j: (0, 0)),
                  pl.BlockSpec((None, dm, tn), lambda l, j: (l, 0, j)),
                  pl.BlockSpec((None, 1, tn), lambda l, j: (l, 0, j))],
        out_specs=pl.BlockSpec((None, 8, tn), lambda l, j: (l, 0, j)),
        out_shape=jax.ShapeDtypeStruct((nl, 8, n3), F32),
        compiler_params=_params("parallel", "parallel"),
        name="ada_mod",
    )(cv, w, b.reshape(nl, 1, n3))


def _row_select(i, tm, n_lat, lat, ctx):
    row = i * tm + lax.broadcasted_iota(jnp.int32, (tm, 1), 0)
    return jnp.where(row < n_lat, lat, ctx)


def _inproj_kernel(x_ref, g_ref, mod_ref, w_ref, o_ref, xm_ref, *, n_lat, tm, dm):
    i = pl.program_id(0)

    @pl.when(pl.program_id(1) == 0)
    def _():
        def chunk(r, carry):
            r0 = pl.multiple_of(r * NORM_ROWS, NORM_ROWS)
            x = x_ref[pl.ds(r0, NORM_ROWS), :]
            ms = jnp.mean(x * x, axis=-1, keepdims=True)
            y = x * lax.rsqrt(ms + EPS) * g_ref[...]
            row = i * tm + r0 + lax.broadcasted_iota(jnp.int32, (NORM_ROWS, 1), 0)
            is_lat = row < n_lat
            shift = jnp.where(is_lat, mod_ref[0:1, 0:dm], mod_ref[1:2, 0:dm])
            scale = jnp.where(is_lat, mod_ref[0:1, dm:2 * dm], mod_ref[1:2, dm:2 * dm])
            xm_ref[pl.ds(r0, NORM_ROWS), :] = (y * (1.0 + scale) + shift).astype(BF)
            return carry
        lax.fori_loop(0, tm // NORM_ROWS, chunk, 0)

    o_ref[...] = jnp.dot(xm_ref[...], w_ref[...], preferred_element_type=F32).astype(BF)


def _inproj(xa, g, mod, w, n_lat):
    s_all, dm = xa.shape
    n = w.shape[1]
    tm = _pick_tile(s_all, 1280, 256)
    tn = 1024
    return pl.pallas_call(
        functools.partial(_inproj_kernel, n_lat=n_lat, tm=tm, dm=dm),
        grid=(s_all // tm, n // tn),
        in_specs=[pl.BlockSpec((tm, dm), lambda i, j: (i, 0)),
                  pl.BlockSpec((1, dm), lambda i, j: (0, 0)),
                  pl.BlockSpec((8, 3 * dm), lambda i, j: (0, 0)),
                  pl.BlockSpec((dm, tn), lambda i, j: (0, j))],
        out_specs=pl.BlockSpec((tm, tn), lambda i, j: (i, j)),
        out_shape=jax.ShapeDtypeStruct((s_all, n), BF),
        scratch_shapes=[pltpu.VMEM((tm, dm), BF)],
        compiler_params=_params("parallel", "arbitrary"),
        name="in_proj",
    )(xa, g.reshape(1, dm), mod, w)


def _outproj_kernel(ya_ref, yb_ref, x_ref, mod_ref, w_ref, o_ref, *, n_lat, tm, dm):
    i = pl.program_id(0)
    y = (jnp.dot(ya_ref[...], w_ref[0:MIX_HALF, :], preferred_element_type=F32)
         + jnp.dot(yb_ref[...], w_ref[MIX_HALF:2 * MIX_HALF, :], preferred_element_type=F32))
    gate = _row_select(i, tm, n_lat, mod_ref[0:1, 2 * dm:3 * dm], mod_ref[1:2, 2 * dm:3 * dm])
    o_ref[...] = x_ref[...] + gate * y


def _outproj(ya, yb, xa, mod, w, n_lat, rows):
    s_all, dm = xa.shape
    tm = _pick_tile(rows, 640, 16)
    return pl.pallas_call(
        functools.partial(_outproj_kernel, n_lat=n_lat, tm=tm, dm=dm),
        grid=(rows // tm,),
        in_specs=[pl.BlockSpec((tm, MIX_HALF), lambda i: (i, 0)),
                  pl.BlockSpec((tm, MIX_HALF), lambda i: (i, 0)),
                  pl.BlockSpec((tm, dm), lambda i: (i, 0)),
                  pl.BlockSpec((8, 3 * dm), lambda i: (0, 0)),
                  pl.BlockSpec((2 * MIX_HALF, dm), lambda i: (0, 0), pipeline_mode=pl.Buffered(1))],
        out_specs=pl.BlockSpec((tm, dm), lambda i: (i, 0)),
        out_shape=jax.ShapeDtypeStruct((rows, dm), F32),
        compiler_params=_params("parallel"),
        name="out_proj",
    )(ya, yb, xa, mod, w)


def _na_bias_table(rpb, rows):
    n_grp = rows // 4
    q_rows, k_rows = NA_TQ // GRID_W, NA_TK // GRID_W
    qc = np.arange(GRID_W)
    cs = np.clip(qc - NA_WIN_W // 2, 0, GRID_W - NA_WIN_W)
    kc = np.arange(GRID_W)
    col_ok = (kc[None, :] >= cs[:, None]) & (kc[None, :] < cs[:, None] + NA_WIN_W)
    off = GRID_W - 1 + NA_WIN_W - 1
    rp = jnp.pad(rpb, ((0, 0), (0, 0), (GRID_W - 1, GRID_W - 1)))
    t2 = jnp.stack([rp[:, :, off - c:off - c + GRID_W] for c in range(GRID_W)], axis=2)
    t2 = jnp.where(col_ok, t2.astype(F32), NEG)
    edge = jnp.full((HEADS, k_rows, GRID_W, GRID_W), NEG, F32)
    t2 = jnp.concatenate([edge, t2, edge], axis=1)
    tables = []
    for g0 in (0, 1, n_grp - 1):
        qr = q_rows * g0 + np.arange(q_rows)
        ks = int(np.clip(q_rows * g0 - NA_WIN_H // 2, 0, rows - k_rows))
        kr = ks + np.arange(k_rows)
        rs = np.clip(qr - NA_WIN_H // 2, 0, rows - NA_WIN_H)
        row_ok = (kr[None, :] >= rs[:, None]) & (kr[None, :] < rs[:, None] + NA_WIN_H)
        per_row = []
        for a in range(q_rows):
            lo = ks - int(qr[a]) + (NA_WIN_H - 1) + k_rows
            blk = jnp.where(row_ok[a][None, :, None, None], t2[:, lo:lo + k_rows], NEG)
            per_row.append(blk.transpose(0, 2, 1, 3))
        tables.append(jnp.stack(per_row, axis=1).reshape(HEADS, NA_TQ, NA_TK))
    tables.append(jnp.full((HEADS, NA_TQ, NA_TK), NEG, F32))
    return jnp.stack(tables, axis=1)


def _na_kernel(q_ref, k_ref, v_ref, g_ref, bias_ref, qg_ref, kg_ref, o_ref, *, n_lat):
    grp = pl.program_id(1)
    ks = pl.multiple_of(jnp.clip((grp - 1) * NA_TQ, 0, n_lat - NA_TK), NA_TQ)

    def norm(t, gain):
        t = t.astype(F32)
        ms = jnp.mean(t * t, axis=-1, keepdims=True)
        return (t * lax.rsqrt(ms + EPS) * gain).astype(BF)

    scale = HEAD_W ** -0.5
    q = norm(q_ref[...], qg_ref[...])
    kl = norm(k_ref[pl.ds(ks, NA_TK), :], kg_ref[...])
    kc = norm(k_ref[n_lat:n_lat + NC, :], kg_ref[...])
    nt = (((1,), (1,)), ((), ()))
    s_loc = lax.dot_general(q, kl, nt, preferred_element_type=F32) * scale + bias_ref[...]
    s_ctx = lax.dot_general(q, kc, nt, preferred_element_type=F32) * scale
    m = jnp.maximum(jnp.max(s_loc, axis=-1, keepdims=True), jnp.max(s_ctx, axis=-1, keepdims=True))
    p_loc = jnp.exp(s_loc - m)
    p_ctx = jnp.exp(s_ctx - m)
    l = jnp.sum(p_loc, axis=-1, keepdims=True) + jnp.sum(p_ctx, axis=-1, keepdims=True)
    o = (jnp.dot(p_loc.astype(BF), v_ref[pl.ds(ks, NA_TK), :], preferred_element_type=F32)
         + jnp.dot(p_ctx.astype(BF), v_ref[n_lat:n_lat + NC, :], preferred_element_type=F32))
    o_ref[...] = ((o / l) * _silu(g_ref[...].astype(F32))).astype(BF)


def _na(z, bias, qn_g, kn_g, n_lat):
    s_all = z.shape[0]
    n_grp = n_lat // NA_TQ

    def variant(h, g):
        return (h, jnp.where(g == 0, 0, jnp.where(g == n_grp - 1, 2, jnp.where(g == n_grp, 3, 1))), 0, 0)

    return pl.pallas_call(
        functools.partial(_na_kernel, n_lat=n_lat),
        grid=(HEADS, s_all // NA_TQ),
        in_specs=[pl.BlockSpec((NA_TQ, HEAD_W), lambda h, g: (g, h)),
                  pl.BlockSpec((s_all, HEAD_W), lambda h, g: (0, HEADS + h)),
                  pl.BlockSpec((s_all, HEAD_W), lambda h, g: (0, 2 * HEADS + h)),
                  pl.BlockSpec((NA_TQ, HEAD_W), lambda h, g: (g, 3 * HEADS + h)),
                  pl.BlockSpec((None, None, NA_TQ, NA_TK), variant),
                  pl.BlockSpec((1, HEAD_W), lambda h, g: (0, 0)),
                  pl.BlockSpec((1, HEAD_W), lambda h, g: (0, 0))],
        out_specs=pl.BlockSpec((NA_TQ, HEAD_W), lambda h, g: (g, h)),
        out_shape=jax.ShapeDtypeStruct((s_all, MIX_HALF), BF),
        compiler_params=_params("parallel", "parallel"),
        name="natten",
    )(z, z, z, z, bias, qn_g.reshape(1, HEAD_W), kn_g.reshape(1, HEAD_W))


def _conv_kernel(ua_ref, ub_ref, uap_ref, ubp_ref, uan_ref, ubn_ref, g_ref, w_ref, b_ref, lg_ref, lb_ref,
                 o_ref, us_ref, *, nt_lat, nt, tt):
    i = pl.program_id(0)
    first = (i == 0) | (i == nt_lat)
    last = (i == nt_lat - 1) | (i == nt - 1)

    def glu(a_ref, b_ref):
        return a_ref[...].astype(F32) * _sigmoid(b_ref[...].astype(F32))

    us_ref[0:HALO, :] = jnp.where(first, 0.0, glu(uap_ref, ubp_ref))
    us_ref[HALO:HALO + tt, :] = glu(ua_ref, ub_ref)
    us_ref[HALO + tt:2 * HALO + tt, :] = jnp.where(last, 0.0, glu(uan_ref, ubn_ref))
    base = HALO - CONV_TAPS // 2
    acc = w_ref[0:1, :] * us_ref[pl.ds(base, tt), :]
    for k in range(1, CONV_TAPS):
        acc = acc + w_ref[k:k + 1, :] * us_ref[pl.ds(base + k, tt), :]
    y = acc + b_ref[...]
    yc = y - jnp.mean(y, axis=-1, keepdims=True)
    var = jnp.mean(yc * yc, axis=-1, keepdims=True)
    yn = yc * lax.rsqrt(var + EPS) * lg_ref[...] + lb_ref[...]
    o_ref[...] = (_silu(yn) * _silu(g_ref[...].astype(F32))).astype(BF)


def _conv(z, w, b, ln_g, ln_b, n_lat):
    s_all = z.shape[0]
    tt = 256
    nt, nt_lat = s_all // tt, n_lat // tt
    hb = tt // HALO
    nh = s_all // HALO
    cw = MIX_HALF
    w_pad = jnp.zeros((32, cw), F32).at[:CONV_TAPS].set(w)
    cur = lambda c: pl.BlockSpec((tt, cw), lambda i: (i, c))
    prev = lambda c: pl.BlockSpec((HALO, cw), lambda i: (jnp.maximum(i * hb - 1, 0), c))
    nxt = lambda c: pl.BlockSpec((HALO, cw), lambda i: (jnp.minimum((i + 1) * hb, nh - 1), c))
    vec = pl.BlockSpec((1, cw), lambda i: (0, 0))
    return pl.pallas_call(
        functools.partial(_conv_kernel, nt_lat=nt_lat, nt=nt, tt=tt),
        grid=(nt,),
        in_specs=[cur(4), cur(5), prev(4), prev(5), nxt(4), nxt(5), cur(6),
                  pl.BlockSpec((32, cw), lambda i: (0, 0)), vec, vec, vec],
        out_specs=pl.BlockSpec((tt, cw), lambda i: (i, 0)),
        out_shape=jax.ShapeDtypeStruct((s_all, cw), BF),
        scratch_shapes=[pltpu.VMEM((tt + 2 * HALO, cw), F32)],
        compiler_params=_params("parallel"),
        name="conformer_conv",
    )(z, z, z, z, z, z, z, w_pad, b.reshape(1, cw), ln_g.reshape(1, cw), ln_b.reshape(1, cw))


def _lru_kernel(*refs, reverse, has_prev, nt_lat, tt):
    if has_prev:
        (u_ref, uh_ref, cw_ref, cb_ref, wa_ref, ba_ref, wx_ref, bx_ref, lam_ref, hp_ref, g_ref,
         o_ref, carry_ref, us_ref, as_ref, bs_ref) = refs
    else:
        (u_ref, uh_ref, cw_ref, cb_ref, wa_ref, ba_ref, wx_ref, bx_ref, lam_ref,
         o_ref, carry_ref, us_ref, as_ref, bs_ref) = refs
    s = pl.program_id(0)
    tile = jnp.where(s == 0, nt_lat, (nt_lat - s) if reverse else (s - 1))

    @pl.when(s == 0)
    def _():
        carry_ref[...] = jnp.zeros_like(carry_ref)

    edge = (tile == nt_lat) | (tile == (nt_lat - 1 if reverse else 0))
    halo = jnp.where(edge, 0.0, uh_ref[...].astype(F32))
    pad = tt // 2
    if reverse:
        us_ref[0:tt, :] = u_ref[...].astype(F32)
        us_ref[tt:tt + HALO, :] = halo
        off = 0
        as_ref[tt:tt + pad, :] = jnp.ones((pad, HEAD_W), F32)
        bs_ref[tt:tt + pad, :] = jnp.zeros((pad, HEAD_W), F32)
        data0 = 0
    else:
        us_ref[0:HALO, :] = halo
        us_ref[HALO:HALO + tt, :] = u_ref[...].astype(F32)
        off = HALO - (LRU_TAPS - 1)
        as_ref[0:pad, :] = jnp.ones((pad, HEAD_W), F32)
        bs_ref[0:pad, :] = jnp.zeros((pad, HEAD_W), F32)
        data0 = pad

    neg_lam = -lam_ref[...]
    softplus = jnp.maximum(neg_lam, 0.0) + jnp.log1p(jnp.exp(-jnp.abs(neg_lam)))
    for blk in range(MIX_HALF // HEAD_W):
        cols = slice(blk * HEAD_W, (blk + 1) * HEAD_W)
        ul = cb_ref[:, cols] + cw_ref[0:1, cols] * us_ref[pl.ds(off, tt), cols]
        for k in range(1, LRU_TAPS):
            ul = ul + cw_ref[k:k + 1, cols] * us_ref[pl.ds(off + k, tt), cols]
        xa = ul.astype(BF)
        r = _sigmoid(jnp.dot(xa, wa_ref[blk], preferred_element_type=F32) + ba_ref[:, cols])
        gi = _sigmoid(jnp.dot(xa, wx_ref[blk], preferred_element_type=F32) + bx_ref[:, cols])
        log_a = -LRU_C * r * softplus[:, cols]
        a = jnp.exp(log_a)
        b = jnp.sqrt(1.0 - jnp.exp(2.0 * log_a)) * (gi * ul)
        d = 1
        while d < tt:
            as_ref[data0:data0 + tt, :] = a
            bs_ref[data0:data0 + tt, :] = b
            sh = data0 + d if reverse else data0 - d
            a_sh = as_ref[pl.ds(sh, tt), :]
            b_sh = bs_ref[pl.ds(sh, tt), :]
            b = a * b_sh + b
            a = a * a_sh
            d *= 2
        h = b + a * carry_ref[0:1, cols]
        h_end = h[0:1, :] if reverse else h[tt - 1:tt, :]
        carry_ref[:, cols] = jnp.broadcast_to(h_end, (8, HEAD_W))
        if has_prev:
            o_ref[:, cols] = ((hp_ref[:, cols] + h) * _silu(g_ref[:, cols].astype(F32))).astype(BF)
        else:
            o_ref[:, cols] = h


def _lru(z, conv_w, conv_b, wa, ba, wx, bx, lam, n_lat, reverse, h_prev=None):
    s_all = z.shape[0]
    tt = 256
    nt, nt_lat = s_all // tt, n_lat // tt
    hb = tt // HALO
    nh = s_all // HALO
    cw = MIX_HALF
    has_prev = h_prev is not None

    def tile(s):
        return jnp.where(s == 0, nt_lat, (nt_lat - s) if reverse else (s - 1))

    if reverse:
        halo_spec = pl.BlockSpec((HALO, cw), lambda s: (jnp.minimum((tile(s) + 1) * hb, nh - 1), 0))
    else:
        halo_spec = pl.BlockSpec((HALO, cw), lambda s: (jnp.maximum(tile(s) * hb - 1, 0), 0))
    vec = pl.BlockSpec((1, cw), lambda s: (0, 0))
    mat = pl.BlockSpec((cw // HEAD_W, HEAD_W, HEAD_W), lambda s: (0, 0, 0))
    in_specs = [pl.BlockSpec((tt, cw), lambda s: (tile(s), 0)), halo_spec,
                pl.BlockSpec((8, cw), lambda s: (0, 0)), vec, mat, vec, mat, vec, vec]
    args = [z, z, jnp.zeros((8, cw), F32).at[:LRU_TAPS].set(conv_w), conv_b.reshape(1, cw),
            wa.astype(BF), ba.reshape(1, cw), wx.astype(BF), bx.reshape(1, cw), lam.reshape(1, cw)]
    if has_prev:
        in_specs += [pl.BlockSpec((tt, cw), lambda s: (tile(s), 0)),
                     pl.BlockSpec((tt, cw), lambda s: (tile(s), 1))]
        args += [h_prev, z]
    return pl.pallas_call(
        functools.partial(_lru_kernel, reverse=reverse, has_prev=has_prev, nt_lat=nt_lat, tt=tt),
        grid=(nt,),
        in_specs=in_specs,
        out_specs=pl.BlockSpec((tt, cw), lambda s: (tile(s), 0)),
        out_shape=jax.ShapeDtypeStruct((s_all, cw), BF if has_prev else F32),
        scratch_shapes=[pltpu.VMEM((8, cw), F32), pltpu.VMEM((tt + HALO, cw), F32),
                        pltpu.VMEM((tt + tt // 2, HEAD_W), F32), pltpu.VMEM((tt + tt // 2, HEAD_W), F32)],
        compiler_params=_params("arbitrary"),
        name="rglru_bwd" if reverse else "rglru_fwd",
    )(*args)


def _rope_tables(n_lat, s_all):
    t = jnp.arange(n_lat, dtype=jnp.int32)
    row = (t // GRID_W).astype(F32)
    col = (t % GRID_W).astype(F32)
    axis_dim = DIFF_DH // 2
    inv_freq = ROPE_BASE ** (-jnp.arange(0, axis_dim, 2, dtype=F32) / axis_dim)
    ang_r = row[:, None] * inv_freq
    ang_c = col[:, None] * inv_freq
    ang = jnp.concatenate([ang_r, ang_r, ang_c, ang_c], axis=-1)
    q4 = DIFF_DH // 4
    sign = np.tile(np.concatenate([-np.ones(q4), np.ones(q4)]), 2).astype(np.float32)
    cos = jnp.tile(jnp.cos(ang), (1, 2))
    sin = jnp.tile(jnp.sin(ang) * sign, (1, 2))
    cos = jnp.concatenate([cos, jnp.ones((s_all - n_lat, HEAD_W), F32)], axis=0)
    sin = jnp.concatenate([sin, jnp.zeros((s_all - n_lat, HEAD_W), F32)], axis=0)
    return cos, sin


def _qkprep_kernel(zq_ref, zk_ref, cos_ref, sin_ref, qg_ref, kg_ref, bd_ref, qp_ref, kp_ref):
    lane = lax.broadcasted_iota(jnp.int32, (1, HEAD_W), 1)
    take_up = (lane % (DIFF_DH // 2)) < (DIFF_DH // 4)
    map0 = lane < DIFF_DH
    cos = cos_ref[...]
    sin = sin_ref[...]
    bd = bd_ref[...]

    def prep(x, gain):
        x = x.astype(F32)
        x2 = x * x
        hi = x2.astype(BF)
        lo = (x2 - hi.astype(F32)).astype(BF)
        ss = jnp.dot(hi, bd, preferred_element_type=F32) + jnp.dot(lo, bd, preferred_element_type=F32)
        xn = x * lax.rsqrt(ss * (1.0 / DIFF_DH) + EPS) * gain
        rot = jnp.where(take_up, pltpu.roll(xn, HEAD_W - DIFF_DH // 4, 1), pltpu.roll(xn, DIFF_DH // 4, 1))
        return xn * cos + rot * sin

    for h in range(HEADS):
        cols = slice(h * HEAD_W, (h + 1) * HEAD_W)
        q = prep(zq_ref[:, cols], qg_ref[...]) * Q_SCALE
        qp_ref[0, :, cols] = jnp.where(map0, q, 0.0).astype(BF)
        qp_ref[1, :, cols] = jnp.where(map0, 0.0, q).astype(BF)
        kp_ref[:, cols] = prep(zk_ref[:, cols], kg_ref[...]).astype(BF)


def _qkprep(z, cos, sin, qn_g, kn_g):
    s_all = z.shape[0]
    tt = _pick_tile(s_all, 640, 16)
    cw = MIX_HALF
    bd = jnp.asarray(np.kron(np.eye(2), np.ones((DIFF_DH, DIFF_DH))), BF)
    vec = pl.BlockSpec((1, HEAD_W), lambda i: (0, 0))
    return pl.pallas_call(
        _qkprep_kernel,
        grid=(s_all // tt,),
        in_specs=[pl.BlockSpec((tt, cw), lambda i: (i, 2)),
                  pl.BlockSpec((tt, cw), lambda i: (i, 3)),
                  pl.BlockSpec((tt, HEAD_W), lambda i: (i, 0)),
                  pl.BlockSpec((tt, HEAD_W), lambda i: (i, 0)),
                  vec, vec,
                  pl.BlockSpec((HEAD_W, HEAD_W), lambda i: (0, 0))],
        out_specs=[pl.BlockSpec((2, tt, cw), lambda i: (0, i, 0)),
                   pl.BlockSpec((tt, cw), lambda i: (i, 0))],
        out_shape=[jax.ShapeDtypeStruct((2, s_all, cw), BF), jax.ShapeDtypeStruct((s_all, cw), BF)],
        compiler_params=_params("parallel"),
        name="qk_prep",
    )(z, z, cos, sin, jnp.tile(qn_g, 2).reshape(1, HEAD_W), jnp.tile(kn_g, 2).reshape(1, HEAD_W), bd)


def _dattn_kernel(q_ref, k_ref, kc_ref, vt_ref, vtc_ref, g_ref, lp_ref, sg_ref, o_ref, s_ref, mx_ref, *,
                  nq, tq, lam_init):
    i = pl.program_id(1)
    q = q_ref[...].reshape(2 * tq, HEAD_W)
    nt = (((1,), (1,)), ((), ()))

    def scores(kk, slot):
        s = lax.dot_general(kk, q, nt, preferred_element_type=F32)
        s_ref[slot, 0:s.shape[0], :] = s
        mx_ref[slot] = jnp.max(s, axis=0, keepdims=True)

    def accumulate(vt, slot, carry):
        m, l, acc = carry
        m_new = jnp.maximum(m, mx_ref[slot])
        alpha = jnp.exp2(m - m_new)
        p = jnp.exp2(s_ref[slot, 0:vt.shape[1], :] - m_new)
        l = alpha * l + jnp.sum(p, axis=0, keepdims=True)
        acc = alpha * acc + jnp.dot(vt, p.astype(BF), preferred_element_type=F32)
        return m_new, l, acc

    init = (jnp.full((1, 2 * tq), NEG, F32), jnp.zeros((1, 2 * tq), F32), jnp.zeros((HEAD_W, 2 * tq), F32))

    def finish(carry):
        _, l, acc = carry
        o = (acc / l).T
        lp = lp_ref[...]
        lam = (jnp.exp(jnp.sum(lp[0:1] * lp[1:2], axis=-1, keepdims=True))
               - jnp.exp(jnp.sum(lp[2:3] * lp[3:4], axis=-1, keepdims=True)) + lam_init)
        o = o[0:tq] - lam * o[tq:2 * tq]
        ms = jnp.mean(o * o, axis=-1, keepdims=True)
        on = o * lax.rsqrt(ms + EPS) * sg_ref[...] * (1.0 - lam_init)
        o_ref[...] = (on * _silu(g_ref[...].astype(F32))).astype(BF)

    @pl.when(i < nq - 1)
    def _():
        n_chunk = k_ref.shape[0]
        pairs = 2 if (n_chunk - 1) % 4 == 0 else 1
        scores(k_ref[0], 0)

        def body(j, carry):
            for u in range(pairs):
                c = 2 * (pairs * j + u)
                scores(k_ref[c + 1], 1)
                carry = accumulate(vt_ref[c], 0, carry)
                scores(k_ref[c + 2], 0)
                carry = accumulate(vt_ref[c + 1], 1, carry)
            return carry

        carry = lax.fori_loop(0, (n_chunk - 1) // (2 * pairs), body, init)
        if n_chunk % 2 == 0:
            scores(k_ref[n_chunk - 1], 1)
            carry = accumulate(vt_ref[n_chunk - 2], 0, carry)
            carry = accumulate(vt_ref[n_chunk - 1], 1, carry)
        else:
            carry = accumulate(vt_ref[n_chunk - 1], 0, carry)
        finish(carry)

    @pl.when(i == nq - 1)
    def _():
        scores(kc_ref[...], 0)
        finish(accumulate(vtc_ref[...], 0, init))


def _dattn(qp, kp, z, lq1, lk1, lq2, lk2, subln_g, n_lat, lam_init):
    s_all = z.shape[0]
    tq = NC
    nq = s_all // tq
    tk = _pick_tile(s_all, 1280, 256)
    n_chunk = s_all // tk
    lp = jnp.zeros((8, HEAD_W), F32).at[0:4, 0:DIFF_DH].set(jnp.stack([lq1, lk1, lq2, lk2]))
    k4 = kp.reshape(n_chunk, tk, HEADS, HEAD_W).transpose(2, 0, 1, 3)
    v = z[:, 4 * MIX_HALF:5 * MIX_HALF]
    vt4 = v.reshape(n_chunk, tk, HEADS, HEAD_W).transpose(2, 0, 3, 1)
    kc = kp[n_lat:].reshape(NC, HEADS, HEAD_W).transpose(1, 0, 2)
    vtc = v[n_lat:].reshape(NC, HEADS, HEAD_W).transpose(1, 2, 0)
    g_col = 5 * HEADS
    return pl.pallas_call(
        functools.partial(_dattn_kernel, nq=nq, tq=tq, lam_init=lam_init),
        grid=(HEADS, nq),
        in_specs=[pl.BlockSpec((2, tq, HEAD_W), lambda h, i: (0, i, h)),
                  pl.BlockSpec((None, n_chunk, tk, HEAD_W), lambda h, i: (h, 0, 0, 0)),
                  pl.BlockSpec((None, NC, HEAD_W), lambda h, i: (h, 0, 0)),
                  pl.BlockSpec((None, n_chunk, HEAD_W, tk), lambda h, i: (h, 0, 0, 0)),
                  pl.BlockSpec((None, HEAD_W, NC), lambda h, i: (h, 0, 0)),
                  pl.BlockSpec((tq, HEAD_W), lambda h, i: (i, g_col + h)),
                  pl.BlockSpec((8, HEAD_W), lambda h, i: (0, 0)),
                  pl.BlockSpec((1, HEAD_W), lambda h, i: (0, 0))],
        out_specs=pl.BlockSpec((tq, HEAD_W), lambda h, i: (i, h)),
        out_shape=jax.ShapeDtypeStruct((s_all, MIX_HALF), BF),
        scratch_shapes=[pltpu.VMEM((2, tk, 2 * tq), F32), pltpu.VMEM((2, 1, 2 * tq), F32)],
        compiler_params=_params("parallel", "parallel"),
        name="diff_attn",
    )(qp, k4, kc, vt4, vtc, z, lp, subln_g.reshape(1, HEAD_W))


def kernel(x, c, ctx, c_ctx, ev_ada_w, ev_ada_b, ev_norm_g, ev_w_in, ev_w_out, ev_na_qn_g, ev_na_kn_g, ev_na_rpb, ev_cv_w, ev_cv_b, ev_cv_ln_g, ev_cv_ln_b, od_ada_w, od_ada_b, od_norm_g, od_w_in, od_w_out, od_lru_conv_w, od_lru_conv_b, od_lru_wa, od_lru_ba, od_lru_wx, od_lru_bx, od_lru_lam, od_df_qn_g, od_df_kn_g, od_df_lq1, od_df_lk1, od_df_lq2, od_df_lk2, od_df_subln_g):
    batch, n_lat, dm = x.shape
    assert batch == 1 and ctx.shape == (1, NC, dm) and dm == 2 * MIX_HALF
    assert n_lat % (4 * NA_TQ) == 0
    s_all = n_lat + NC
    depth = ev_w_in.shape[0] + od_w_in.shape[0]

    xa = jnp.concatenate([x[0], ctx[0]], axis=0)
    cv = jnp.zeros((8, dm), F32).at[0].set(c[0]).at[1].set(c_ctx)
    mods_ev = _ada(cv, ev_ada_w, ev_ada_b)
    mods_od = _ada(cv, od_ada_w, od_ada_b)
    cos, sin = _rope_tables(n_lat, s_all)

    for l in range(depth):
        i = l // 2
        rows = n_lat if l == depth - 1 else s_all
        if l % 2 == 0:
            z = _inproj(xa, ev_norm_g[i], mods_ev[i], ev_w_in[i].astype(BF), n_lat)
            bias = _na_bias_table(ev_na_rpb[i], n_lat // GRID_W)
            ya = _na(z, bias, ev_na_qn_g[i], ev_na_kn_g[i], n_lat)
            yb = _conv(z, ev_cv_w[i], ev_cv_b[i], ev_cv_ln_g[i], ev_cv_ln_b[i], n_lat)
            xa = _outproj(ya, yb, xa, mods_ev[i], ev_w_out[i].astype(BF), n_lat, rows)
        else:
            z = _inproj(xa, od_norm_g[i], mods_od[i], od_w_in[i].astype(BF), n_lat)
            lru = lambda d, rev, hp: _lru(
                z, od_lru_conv_w[i, d], od_lru_conv_b[i, d], od_lru_wa[i, d], od_lru_ba[i, d],
                od_lru_wx[i, d], od_lru_bx[i, d], od_lru_lam[i, d], n_lat, rev, hp)
            ya = lru(1, True, lru(0, False, None))
            qp, kp = _qkprep(z, cos, sin, od_df_qn_g[i], od_df_kn_g[i])
            lam_init = 0.8 - 0.6 * math.exp(-0.3 * l)
            yb = _dattn(qp, kp, z, od_df_lq1[i], od_df_lk1[i], od_df_lq2[i], od_df_lk2[i],
                        od_df_subln_g[i], n_lat, lam_init)
            xa = _outproj(ya, yb, xa, mods_od[i], od_w_out[i].astype(BF), n_lat, rows)
    return xa[None]
```
